```python
import math
import jax, jax.numpy as jnp
from jax import lax
import numpy as np

D_MODEL = 1024
BATCH = 16
SEQ = 2048
DEPTH = 2

CTX_LEN = 256
GRID_W = 64

CONV_CH = 256
CONV_WIDTH = 31
DIFF_HEADS = 4
DIFF_DH = 32
FOURIER_GROUPS = 4
FOURIER_CH = 64
WIN_Q_HEADS = 4
WIN_KV_HEADS = 2
WIN_REP = WIN_Q_HEADS // WIN_KV_HEADS
WIN_DH = 64
WINDOW = 128
BLOCK = 128
ROPE_BASE = 10000.0

DIFF_QK = DIFF_HEADS * 2 * DIFF_DH
DIFF_V = DIFF_HEADS * 2 * DIFF_DH
FOURIER_W = FOURIER_GROUPS * FOURIER_CH
WIN_Q = WIN_Q_HEADS * WIN_DH
WIN_KV = WIN_KV_HEADS * WIN_DH
SPLIT_SIZES = (CONV_CH, CONV_CH, DIFF_QK, DIFF_QK, DIFF_V, FOURIER_W, WIN_Q, WIN_KV, WIN_KV)
D_IN = 2 * CONV_CH + 2 * DIFF_QK + DIFF_V + FOURIER_W + WIN_Q + 2 * WIN_KV
D_MIX = CONV_CH + 2 * DIFF_HEADS * DIFF_DH + FOURIER_W + WIN_Q

N_EXPERTS = 16
N_EXPERT_GROUPS = 4
EXPERTS_PER_GROUP = N_EXPERTS // N_EXPERT_GROUPS
TOP_K = 2
D_EXPERT = 512

ALPHA = (2 * DEPTH) ** 0.25
BETA = (8 * DEPTH) ** -0.25
LN_EPS = 1e-5

kernel_name = "hybrid_parallel_mixer_dit_block"


def layer_norm(x, g=None, b=None):
    xf = x.astype(jnp.float32)
    mu = jnp.mean(xf, -1, keepdims=True)
    var = jnp.mean(jnp.square(xf - mu), -1, keepdims=True)
    y = (xf - mu) * lax.rsqrt(var + LN_EPS)
    if g is not None:
        y = y * g.astype(jnp.float32) + b.astype(jnp.float32)
    return y.astype(x.dtype)


def rms_norm(x, g):
    xf = x.astype(jnp.float32)
    y = xf * lax.rsqrt(jnp.mean(xf * xf, -1, keepdims=True) + LN_EPS) * g.astype(jnp.float32)
    return y.astype(x.dtype)


def modulate(x, shift, scale):
    return layer_norm(x) * (1.0 + scale) + shift


def split_columns(u):
    idx = np.cumsum(SPLIT_SIZES)[:-1].tolist()
    return jnp.split(u, idx, axis=-1)


def axial_rope_tables(row_pos, col_pos, head_dim):
    n_axis = head_dim // 4
    inv = ROPE_BASE ** (-jnp.arange(n_axis, dtype=jnp.float32) / n_axis)
    ang = jnp.concatenate([row_pos[:, None] * inv, col_pos[:, None] * inv], -1)
    return jnp.cos(ang), jnp.sin(ang)


def apply_rope(x, cos, sin):
    xf = x.astype(jnp.float32)
    x1, x2 = xf[..., 0::2], xf[..., 1::2]
    y = jnp.stack([x1 * cos - x2 * sin, x1 * sin + x2 * cos], -1).reshape(x.shape)
    return y.astype(x.dtype)


def conv_module(u_val, u_gate, w_dw, b_dw, g_n, b_n):
    g = u_val * jax.nn.sigmoid(u_gate)
    y = lax.conv_general_dilated(
        g, w_dw[:, None, :].astype(g.dtype), window_strides=(1,),
        padding=[(CONV_WIDTH // 2, CONV_WIDTH // 2)],
        dimension_numbers=("NWC", "WIO", "NWC"), feature_group_count=CONV_CH)
    y = y + b_dw
    return jax.nn.silu(layer_norm(y, g_n, b_n))


def _diff_attend(q, k, v, lam):
    s = jnp.einsum("bqhcd,bkhcd->bhcqk", q, k).astype(jnp.float32) * (DIFF_DH ** -0.5)
    p = jax.nn.softmax(s, axis=-1)
    a = p[:, :, 0] - lam * p[:, :, 1]
    return jnp.einsum("bhqk,bkhe->bqhe", a.astype(v.dtype), v)


def diff_attention_latent(q, k, v, kc, vc, lam, cos, sin):
    B_, N = q.shape[0], q.shape[1]
    cs, sn = cos[:, None, None, :], sin[:, None, None, :]
    q = apply_rope(q, cs, sn)
    k = apply_rope(k, cs, sn)
    k_all = jnp.concatenate([kc, k], axis=1)
    v_all = jnp.concatenate([vc, v], axis=1)
    nb = N // BLOCK
    q_blocks = jnp.moveaxis(q.reshape(B_, nb, BLOCK, DIFF_HEADS, 2, DIFF_DH), 1, 0)
    o = lax.map(lambda qb: _diff_attend(qb, k_all, v_all, lam), q_blocks)
    return jnp.moveaxis(o, 0, 1).reshape(B_, N, DIFF_HEADS, 2 * DIFF_DH)


def diff_finish(o, lam_init, subln_g):
    o = rms_norm(o, subln_g) * (1.0 - lam_init)
    return o.reshape(o.shape[0], o.shape[1], DIFF_HEADS * 2 * DIFF_DH)


def fourier_mix(u):
    B_, N, _ = u.shape
    z = u.astype(jnp.float32).reshape(B_, N, FOURIER_GROUPS, FOURIER_CH)
    y = jnp.fft.fft2(z, axes=(1, 3), norm="ortho").real
    return y.reshape(B_, N, FOURIER_W).astype(u.dtype)


def window_attention_latent(q, k, v, kc, vc, sink, cos, sin):
    B_, N = q.shape[0], q.shape[1]
    C = kc.shape[1]
    nb = N // BLOCK
    cs, sn = cos[:, None, :], sin[:, None, :]
    q = apply_rope(q, cs, sn)
    k = apply_rope(k, cs, sn)
    qg = q.reshape(B_, nb, BLOCK, WIN_KV_HEADS, WIN_REP, WIN_DH)
    pad = ((0, 0), (BLOCK, BLOCK), (0, 0), (0, 0))
    kb = jnp.pad(k, pad).reshape(B_, nb + 2, BLOCK, WIN_KV_HEADS, WIN_DH)
    vb = jnp.pad(v, pad).reshape(B_, nb + 2, BLOCK, WIN_KV_HEADS, WIN_DH)
    kw = jnp.concatenate([kb[:, :-2], kb[:, 1:-1], kb[:, 2:]], axis=2)
    vw = jnp.concatenate([vb[:, :-2], vb[:, 1:-1], vb[:, 2:]], axis=2)
    qpos = jnp.arange(nb)[:, None] * BLOCK + jnp.arange(BLOCK)[None, :]
    kpos = (jnp.arange(nb)[:, None] - 1) * BLOCK + jnp.arange(3 * BLOCK)[None, :]
    rel = kpos[:, None, :] - qpos[:, :, None]
    valid = (jnp.abs(rel) <= WINDOW) & (kpos[:, None, :] >= 0) & (kpos[:, None, :] < N)
    scale = WIN_DH ** -0.5
    s_loc = jnp.einsum("bnqgrd,bnkgd->bngrqk", qg, kw).astype(jnp.float32) * scale
    s_loc = jnp.where(valid[None, :, None, None], s_loc, -jnp.inf)
    s_ctx = jnp.einsum("bnqgrd,bcgd->bngrqc", qg, kc).astype(jnp.float32) * scale
    s_sink = jnp.broadcast_to(
        sink.astype(jnp.float32).reshape(1, 1, WIN_KV_HEADS, WIN_REP, 1, 1), s_loc.shape[:-1] + (1,))
    p = jax.nn.softmax(jnp.concatenate([s_loc, s_ctx, s_sink], -1), axis=-1)
    L = 3 * BLOCK
    o = (jnp.einsum("bngrqk,bnkgd->bnqgrd", p[..., :L].astype(v.dtype), vw)
         + jnp.einsum("bngrqc,bcgd->bnqgrd", p[..., L:L + C].astype(v.dtype), vc))
    return o.reshape(B_, N, WIN_Q)


def window_attention_context(qc, kc, vc, sink):
    B_, C = qc.shape[0], qc.shape[1]
    qg = qc.reshape(B_, C, WIN_KV_HEADS, WIN_REP, WIN_DH)
    s = jnp.einsum("bqgrd,bkgd->bgrqk", qg, kc).astype(jnp.float32) * (WIN_DH ** -0.5)
    s_sink = jnp.broadcast_to(sink.astype(jnp.float32).reshape(WIN_KV_HEADS, WIN_REP, 1, 1), s.shape[:-1] + (1,))
    p = jax.nn.softmax(jnp.concatenate([s, s_sink], -1), axis=-1)
    o = jnp.einsum("bgrqk,bkgd->bqgrd", p[..., :C].astype(vc.dtype), vc)
    return o.reshape(B_, C, WIN_Q)


def token_mixing(h, hc, w_in, w_out, conv_w, conv_b, conv_ng, conv_nb, lam, lam_init, subln_g, sink,
                 rope_d, rope_w, need_ctx):
    B_, N, _ = h.shape
    C = hc.shape[1]
    av, ag, dq, dk, dv, fz, wq, wk, wv = split_columns(h @ w_in)
    avc, agc, dqc, dkc, dvc, fzc, wqc, wkc, wvc = split_columns(hc @ w_in)
    dq = dq.reshape(B_, N, DIFF_HEADS, 2, DIFF_DH)
    dk = dk.reshape(B_, N, DIFF_HEADS, 2, DIFF_DH)
    dv = dv.reshape(B_, N, DIFF_HEADS, 2 * DIFF_DH)
    dqc = dqc.reshape(B_, C, DIFF_HEADS, 2, DIFF_DH)
    dkc = dkc.reshape(B_, C, DIFF_HEADS, 2, DIFF_DH)
    dvc = dvc.reshape(B_, C, DIFF_HEADS, 2 * DIFF_DH)
    wq = wq.reshape(B_, N, WIN_Q_HEADS, WIN_DH)
    wk = wk.reshape(B_, N, WIN_KV_HEADS, WIN_DH)
    wv = wv.reshape(B_, N, WIN_KV_HEADS, WIN_DH)
    wkc = wkc.reshape(B_, C, WIN_KV_HEADS, WIN_DH)
    wvc = wvc.reshape(B_, C, WIN_KV_HEADS, WIN_DH)

    y_conv = conv_module(av, ag, conv_w, conv_b, conv_ng, conv_nb)
    y_diff = diff_finish(diff_attention_latent(dq, dk, dv, dkc, dvc, lam, rope_d[0], rope_d[1]), lam_init, subln_g)
    y_four = fourier_mix(fz)
    y_win = window_attention_latent(wq, wk, wv, wkc, wvc, sink, rope_w[0], rope_w[1])
    y = jnp.concatenate([y_conv, y_diff, y_four, y_win], axis=-1) @ w_out
    if not need_ctx:
        return y, None

    yc_conv = conv_module(avc, agc, conv_w, conv_b, conv_ng, conv_nb)
    yc_diff = diff_finish(_diff_attend(dqc, dkc, dvc, lam), lam_init, subln_g)
    yc_four = fourier_mix(fzc)
    yc_win = window_attention_context(wqc.reshape(B_, C, WIN_Q_HEADS, WIN_DH), wkc, wvc, sink)
    yc = jnp.concatenate([yc_conv, yc_diff, yc_four, yc_win], axis=-1) @ w_out
    return y, yc


def moe_ffn(h, router_w, router_bias, w_gate, w_up, w_down):
    shp = h.shape
    t = h.reshape(-1, shp[-1])
    scores = jax.nn.sigmoid((t @ router_w).astype(jnp.float32))
    sel = scores + router_bias.astype(jnp.float32)
    grp = sel.reshape(-1, N_EXPERT_GROUPS, EXPERTS_PER_GROUP)
    grp_score = jnp.sum(lax.top_k(grp, TOP_K)[0], -1)
    best = jnp.argmax(grp_score, -1)
    in_grp = jnp.arange(N_EXPERT_GROUPS)[None, :] == best[:, None]
    masked = jnp.where(jnp.repeat(in_grp, EXPERTS_PER_GROUP, axis=-1), sel, -jnp.inf)
    _, idx = lax.top_k(masked, TOP_K)
    w = jnp.take_along_axis(scores, idx, -1)
    w = w / jnp.sum(w, -1, keepdims=True)
    combine = jnp.sum(jax.nn.one_hot(idx, N_EXPERTS, dtype=jnp.float32) * w[..., None], axis=1)
    combine = combine.astype(t.dtype)
    out = jnp.zeros_like(t)
    for e in range(N_EXPERTS):
        he = jax.nn.silu(t @ w_gate[e]) * (t @ w_up[e])
        out = out + combine[:, e:e + 1] * (he @ w_down[e])
    return out.reshape(shp)


def setup_inputs(seed: int = 0) -> dict:
    key = jax.random.key(seed)
    ks = jax.random.split(key, 24)
    f32 = jnp.float32
    D, E, F = D_MODEL, N_EXPERTS, D_EXPERT

    def nrm(k, shape, scale):
        return jax.random.normal(k, shape, f32) * scale

    return {
        "x": nrm(ks[0], (BATCH, SEQ, D), 1.0),
        "c": nrm(ks[1], (BATCH, D), 1.0),
        "ctx": nrm(ks[2], (BATCH, CTX_LEN, D), 1.0),
        "c_ctx": nrm(ks[3], (D,), 1.0),
        "w_mod": nrm(ks[4], (DEPTH, D, 6 * D), D ** -0.5),
        "b_mod": nrm(ks[5], (DEPTH, 6 * D), 0.02),
        "w_in": nrm(ks[6], (DEPTH, D, D_IN), D ** -0.5),
        "w_out": nrm(ks[7], (DEPTH, D_MIX, D), D_MIX ** -0.5 * BETA),
        "conv_w": nrm(ks[8], (DEPTH, CONV_WIDTH, CONV_CH), CONV_WIDTH ** -0.5),
        "conv_b": nrm(ks[9], (DEPTH, CONV_CH), 0.02),
        "conv_norm_g": 1.0 + nrm(ks[10], (DEPTH, CONV_CH), 0.02),
        "conv_norm_b": nrm(ks[11], (DEPTH, CONV_CH), 0.02),
        "diff_lambda": nrm(ks[12], (DEPTH, 4, DIFF_DH), 0.1),
        "diff_subln_g": 1.0 + nrm(ks[13], (DEPTH, 2 * DIFF_DH), 0.02),
        "win_sink": nrm(ks[14], (DEPTH, WIN_Q_HEADS), 0.5),
        "ln_mix_g": 1.0 + nrm(ks[15], (DEPTH, D), 0.02),
        "ln_mix_b": nrm(ks[16], (DEPTH, D), 0.02),
        "ln_ffn_g": 1.0 + nrm(ks[17], (DEPTH, D), 0.02),
        "ln_ffn_b": nrm(ks[18], (DEPTH, D), 0.02),
        "router_w": nrm(ks[19], (D, E), D ** -0.5),
        "router_bias": nrm(ks[20], (E,), 0.01),
        "exp_w_gate": nrm(ks[21], (DEPTH, E, D, F), D ** -0.5),
        "exp_w_up": nrm(ks[22], (DEPTH, E, D, F), D ** -0.5),
        "exp_w_down": nrm(ks[23], (DEPTH, E, F, D), F ** -0.5 * BETA),
    }


def reference(x, c, ctx, c_ctx, w_mod, b_mod, w_in, w_out, conv_w, conv_b, conv_norm_g, conv_norm_b,
              diff_lambda, diff_subln_g, win_sink, ln_mix_g, ln_mix_b, ln_ffn_g, ln_ffn_b,
              router_w, router_bias, exp_w_gate, exp_w_up, exp_w_down):
    n_lat = x.shape[1]
    ROWS = n_lat // GRID_W
    row_pos = jnp.repeat(jnp.arange(ROWS, dtype=jnp.float32), GRID_W)
    col_pos = jnp.tile(jnp.arange(GRID_W, dtype=jnp.float32), ROWS)
    rope_d = axial_rope_tables(row_pos, col_pos, DIFF_DH)
    rope_w = axial_rope_tables(row_pos, col_pos, WIN_DH)

    xc = ctx
    for l in range(DEPTH):
        need_ctx = l < DEPTH - 1
        mod = jax.nn.silu(c) @ w_mod[l] + b_mod[l]
        mod_c = jax.nn.silu(c_ctx) @ w_mod[l] + b_mod[l]
        sh1, sc1, g1, sh2, sc2, g2 = jnp.split(mod[:, None, :], 6, axis=-1)
        csh1, csc1, cg1, csh2, csc2, cg2 = jnp.split(mod_c, 6, axis=-1)

        lam_init = 0.8 - 0.6 * math.exp(-0.3 * l)
        lv = diff_lambda[l].astype(jnp.float32)
        lam = jnp.exp(jnp.sum(lv[0] * lv[1])) - jnp.exp(jnp.sum(lv[2] * lv[3])) + lam_init

        h = modulate(x, sh1, sc1)
        hc = modulate(xc, csh1, csc1)
        y, yc = token_mixing(h, hc, w_in[l], w_out[l], conv_w[l], conv_b[l], conv_norm_g[l], conv_norm_b[l],
                             lam, lam_init, diff_subln_g[l], win_sink[l], rope_d, rope_w, need_ctx)
        x = layer_norm(ALPHA * x + g1 * y, ln_mix_g[l], ln_mix_b[l])
        h = modulate(x, sh2, sc2)
        x = layer_norm(ALPHA * x + g2 * moe_ffn(h, router_w, router_bias, exp_w_gate[l], exp_w_up[l], exp_w_down[l]),
                       ln_ffn_g[l], ln_ffn_b[l])
        if need_ctx:
            xc = layer_norm(ALPHA * xc + cg1 * yc, ln_mix_g[l], ln_mix_b[l])
            hc = modulate(xc, csh2, csc2)
            xc = layer_norm(ALPHA * xc + cg2 * moe_ffn(hc, router_w, router_bias, exp_w_gate[l], exp_w_up[l], exp_w_down[l]),
                            ln_ffn_g[l], ln_ffn_b[l])
    return x
```

```python
import functools
import math

import numpy as np
import jax
import jax.numpy as jnp
from jax import lax
from jax.experimental import pallas as pl
from jax.experimental.pallas import tpu as pltpu

F32 = jnp.float32
BF16 = jnp.bfloat16

D_MODEL = 1024
DEPTH = 2
GRID_W = 64
CONV_CH = 256
CONV_WIDTH = 31
DIFF_HEADS = 4
DIFF_DH = 32
FOURIER_GROUPS = 4
FOURIER_CH = 64
WIN_DH = 64
WIN_KV_HEADS = 2
WINDOW = 128
ROPE_BASE = 10000.0
N_EXPERTS = 16
N_EXPERT_GROUPS = 4
EXPERTS_PER_GROUP = 4
D_EXPERT = 512
ALPHA = (2 * DEPTH) ** 0.25
LN_EPS = 1e-5

U_AV, U_AG, U_DQ, U_DK, U_DV, U_FZ, U_WQ, U_WK, U_WV = (256 * i for i in range(9))
U_WIDTH = 9 * 256
ROPE_GROUPS = (U_DQ, U_DK, U_WQ, U_WK)
V7X_VMEM_LIMIT = 48 * 1024 * 1024


def _cparams(*sem):
    return pltpu.CompilerParams(dimension_semantics=sem, vmem_limit_bytes=V7X_VMEM_LIMIT)


def _ln(x):
    mu = jnp.mean(x, axis=-1, keepdims=True)
    xc = x - mu
    var = jnp.mean(xc * xc, axis=-1, keepdims=True)
    return xc * lax.rsqrt(var + LN_EPS)


def _sigmoid(x):
    return 1.0 / (1.0 + jnp.exp(-x))


def _dot(a, b):
    return jnp.dot(a, b, preferred_element_type=F32)


def _dot_nt(a, b):
    return lax.dot_general(a, b, (((1,), (1,)), ((), ())), preferred_element_type=F32)


def _mod_kernel(c_ref, w_ref, b_ref, o_ref):
    c = c_ref[...]
    s = c * _sigmoid(c)
    o_ref[0] = jnp.dot(s, w_ref[0], preferred_element_type=F32, precision=lax.Precision.HIGHEST) + b_ref[0]


def _mod_call(cs, w_mod, b_mod):
    R, D = cs.shape
    return pl.pallas_call(
        _mod_kernel,
        grid=(DEPTH, 6),
        in_specs=[pl.BlockSpec((R, D), lambda l, j: (0, 0)),
                  pl.BlockSpec((1, D, D), lambda l, j: (l, 0, j)),
                  pl.BlockSpec((1, 1, D), lambda l, j: (l, 0, j))],
        out_specs=pl.BlockSpec((1, R, D), lambda l, j: (l, 0, j)),
        out_shape=jax.ShapeDtypeStruct((DEPTH, R, 6 * D), F32),
        compiler_params=_cparams("arbitrary", "arbitrary"),
        name="mod",
    )(cs, w_mod, b_mod.reshape(DEPTH, 1, 6 * D))


def _inproj_kernel(*refs, rope):
    if rope:
        x_ref, sh_ref, sc_ref, w_ref, cd_ref, sd_ref, cw_ref, sw_ref, o_ref = refs
    else:
        x_ref, sh_ref, sc_ref, w_ref, o_ref = refs
    h = (_ln(x_ref[0]) * (1.0 + sc_ref[0]) + sh_ref[0]).astype(BF16)
    n_sw = 0
    for a in range(0, U_WIDTH, 256):
        val = _dot(h, w_ref[:, a:a + 256])
        if a in (U_DQ, U_WQ):
            val = val * (DIFF_DH ** -0.5 if a == U_DQ else WIN_DH ** -0.5)
        if rope and a in ROPE_GROUPS:
            sw = _dot(h, w_ref[:, U_WIDTH + 256 * n_sw:U_WIDTH + 256 * (n_sw + 1)])
            if a in (U_DQ, U_WQ):
                sw = sw * (DIFF_DH ** -0.5 if a == U_DQ else WIN_DH ** -0.5)
            n_sw += 1
            cos, sin = (cd_ref, sd_ref) if a in (U_DQ, U_DK) else (cw_ref, sw_ref)
            val = val * cos[...] + sw * sin[...]
        o_ref[0, :, a:a + 256] = val.astype(BF16)


def _inproj_call(x, mod, mod_row, w, tables, tn):
    B, N, D = x.shape
    rope = tables is not None
    row = (lambda b: b) if mod_row is None else (lambda b: mod_row)
    in_specs = [pl.BlockSpec((1, tn, D), lambda b, i: (b, i, 0)),
                pl.BlockSpec((1, 1, D), lambda b, i: (row(b), 0, 0)),
                pl.BlockSpec((1, 1, D), lambda b, i: (row(b), 0, 1)),
                pl.BlockSpec(w.shape, lambda b, i: (0, 0))]
    args = [x, mod, mod, w]
    if rope:
        in_specs += [pl.BlockSpec((tn, 256), lambda b, i: (i, 0))] * 4
        args += list(tables)
    return pl.pallas_call(
        functools.partial(_inproj_kernel, rope=rope),
        grid=(B, N // tn),
        in_specs=in_specs,
        out_specs=pl.BlockSpec((1, tn, U_WIDTH), lambda b, i: (b, i, 0)),
        out_shape=jax.ShapeDtypeStruct((B, N, U_WIDTH), BF16),
        compiler_params=_cparams("parallel", "parallel"),
        name="inproj_rope" if rope else "inproj",
    )(*args)


CONV_ROWS = 128
CONV_PAD = 16


def _conv_kernel(u_ref, w_ref, b_ref, g_ref, nb_ref, o_ref, pad_ref, sh_ref):
    N = u_ref.shape[1]
    L = N + 2 * CONV_PAD
    val = u_ref[0, :, 0:CONV_CH].astype(F32)
    gate = u_ref[0, :, CONV_CH:2 * CONV_CH].astype(F32)
    pad_ref[0:CONV_PAD, :] = jnp.zeros((CONV_PAD, CONV_CH), F32)
    pad_ref[CONV_PAD + N:L + 8, :] = jnp.zeros((CONV_PAD + 8, CONV_CH), F32)
    pad_ref[CONV_PAD:CONV_PAD + N, :] = val * _sigmoid(gate)
    for s in range(8):
        sh_ref[s] = pad_ref[s:s + L, :]

    def tile(i, carry):
        r0 = pl.multiple_of(i * CONV_ROWS, CONV_ROWS)
        acc = jnp.zeros((CONV_ROWS, CONV_CH), F32)
        for k in range(CONV_WIDTH):
            off = CONV_PAD - CONV_WIDTH // 2 + k
            acc = acc + w_ref[k:k + 1, :] * sh_ref[off % 8, pl.ds(r0 + 8 * (off // 8), CONV_ROWS), :]
        y = _ln(acc + b_ref[...]) * g_ref[...] + nb_ref[...]
        o_ref[0, pl.ds(r0, CONV_ROWS), :] = (y * _sigmoid(y)).astype(BF16)
        return carry

    lax.fori_loop(0, N // CONV_ROWS, tile, 0)


def _conv_call(u, conv_w, conv_b, conv_ng, conv_nb):
    B, N, _ = u.shape
    vec = pl.BlockSpec((1, CONV_CH), lambda b: (0, 0))
    return pl.pallas_call(
        _conv_kernel,
        grid=(B,),
        in_specs=[pl.BlockSpec((1, N, 2 * CONV_CH), lambda b: (b, 0, 0)),
                  pl.BlockSpec((CONV_WIDTH, CONV_CH), lambda b: (0, 0)), vec, vec, vec],
        out_specs=pl.BlockSpec((1, N, CONV_CH), lambda b: (b, 0, 0)),
        out_shape=jax.ShapeDtypeStruct((B, N, CONV_CH), BF16),
        scratch_shapes=[pltpu.VMEM((N + 2 * CONV_PAD + 8, CONV_CH), F32),
                        pltpu.VMEM((8, N + 2 * CONV_PAD, CONV_CH), F32)],
        compiler_params=_cparams("parallel"),
        name="conv",
    )(u, conv_w, conv_b.reshape(1, -1), conv_ng.reshape(1, -1), conv_nb.reshape(1, -1))


def _diff_kernel(*refs, lam_init, has_lat):
    if has_lat:
        q_ref, k_ref, v_ref, kc_ref, vc_ref, dl_ref, g_ref, o_ref = refs
    else:
        q_ref, kc_ref, vc_ref, dl_ref, g_ref, o_ref = refs
    q = q_ref[0]
    tq = q.shape[0]
    kc, vc = kc_ref[0], vc_ref[0]
    if has_lat:
        k, v = k_ref[0], v_ref[0]
    dl = dl_ref[...]
    lam = (jnp.exp(jnp.sum(dl[0:1] * dl[1:2], axis=-1, keepdims=True))
           - jnp.exp(jnp.sum(dl[2:3] * dl[3:4], axis=-1, keepdims=True)) + lam_init)
    lane = lax.broadcasted_iota(jnp.int32, (1, DIFF_HEADS * 2 * DIFF_DH), 1)
    o = jnp.zeros((tq, DIFF_HEADS * 2 * DIFF_DH), F32)
    head_masks = []
    for h in range(DIFF_HEADS):
        probs = []
        for c in range(2):
            lo = (2 * h + c) * DIFF_DH
            qm = jnp.where((lane >= lo) & (lane < lo + DIFF_DH), q, jnp.zeros_like(q))
            s_c = _dot_nt(qm, kc)
            m = jnp.max(s_c, axis=-1, keepdims=True)
            if has_lat:
                s_l = _dot_nt(qm, k)
                m = jnp.maximum(m, jnp.max(s_l, axis=-1, keepdims=True))
            e_c = jnp.exp(s_c - m)
            den = jnp.sum(e_c, axis=-1, keepdims=True)
            e_l = None
            if has_lat:
                e_l = jnp.exp(s_l - m)
                den = den + jnp.sum(e_l, axis=-1, keepdims=True)
            probs.append((e_c, e_l, 1.0 / den))
        (e_c0, e_l0, i0), (e_c1, e_l1, i1) = probs
        i1 = lam * i1
        oh = _dot((e_c0 * i0 - e_c1 * i1).astype(BF16), vc)
        if has_lat:
            oh = oh + _dot((e_l0 * i0 - e_l1 * i1).astype(BF16), v)
        hm = (lane >= 2 * h * DIFF_DH) & (lane < (2 * h + 2) * DIFF_DH)
        head_masks.append(hm)
        o = o + jnp.where(hm, oh, 0.0)
    o2 = o * o
    ms = jnp.zeros_like(o)
    for hm in head_masks:
        ssq = jnp.sum(jnp.where(hm, o2, 0.0), axis=-1, keepdims=True) * (1.0 / (2 * DIFF_DH))
        ms = ms + jnp.where(hm, ssq, 0.0)
    y = o * lax.rsqrt(ms + LN_EPS) * g_ref[...] * (1.0 - lam_init)
    o_ref[0] = y.astype(BF16)


def _diff_call(u, uc, diff_lambda, subln_g, lam_init, tq):
    has_lat = u is not None
    src = u if has_lat else uc
    B, N, _ = src.shape
    C = uc.shape[1]
    W = DIFF_HEADS * 2 * DIFF_DH
    blk = lambda col: U_DQ // 256 + col
    in_specs = [pl.BlockSpec((1, tq, W), lambda b, i: (b, i, blk(0)))]
    args = [src]
    if has_lat:
        in_specs += [pl.BlockSpec((1, N, W), lambda b, i: (b, 0, blk(1))),
                     pl.BlockSpec((1, N, W), lambda b, i: (b, 0, blk(2)))]
        args += [u, u]
    in_specs += [pl.BlockSpec((1, C, W), lambda b, i: (b, 0, blk(1))),
                 pl.BlockSpec((1, C, W), lambda b, i: (b, 0, blk(2))),
                 pl.BlockSpec((4, DIFF_DH), lambda b, i: (0, 0)),
                 pl.BlockSpec((1, W), lambda b, i: (0, 0))]
    args += [uc, uc, diff_lambda, jnp.tile(subln_g, DIFF_HEADS).reshape(1, W)]
    return pl.pallas_call(
        functools.partial(_diff_kernel, lam_init=lam_init, has_lat=has_lat),
        grid=(B, N // tq),
        in_specs=in_specs,
        out_specs=pl.BlockSpec((1, tq, W), lambda b, i: (b, i, 0)),
        out_shape=jax.ShapeDtypeStruct((B, N, W), BF16),
        compiler_params=_cparams("parallel", "parallel"),
        name="diff_attn" if has_lat else "diff_attn_ctx",
    )(*args)


def _fourier_kernel(z_ref, cs_ref, cn_ref, sn_ref, o_ref, a_ref, b_ref, *, scale):
    W = FOURIER_GROUPS * FOURIER_CH

    @pl.when(pl.program_id(1) == 0)
    def _():
        t = _dot(z_ref[0], cs_ref[...])
        a_ref[...] = t[:, :W].astype(BF16)
        b_ref[...] = t[:, W:].astype(BF16)

    y = _dot(cn_ref[...], a_ref[...]) - _dot(sn_ref[...], b_ref[...])
    o_ref[0] = (y * scale).astype(BF16)


def _dft_tables(N):
    n = jnp.arange(N, dtype=jnp.int32)
    ang = ((n[:, None] * n[None, :]) % N).astype(F32) * (2.0 * math.pi / N)
    c = np.arange(FOURIER_CH, dtype=np.int64)
    angc = 2.0 * np.pi * ((c[:, None] * c[None, :]) % FOURIER_CH).astype(np.float64) / FOURIER_CH
    eye = np.eye(FOURIER_GROUPS)
    cs = np.concatenate([np.kron(eye, np.cos(angc)), np.kron(eye, np.sin(angc))], axis=1)
    return jnp.asarray(cs, F32).astype(BF16), jnp.cos(ang).astype(BF16), jnp.sin(ang).astype(BF16)


def _fourier_call(u, dft, tm):
    B, N, _ = u.shape
    W = FOURIER_GROUPS * FOURIER_CH
    cs, cn, sn = dft
    return pl.pallas_call(
        functools.partial(_fourier_kernel, scale=float((N * FOURIER_CH) ** -0.5)),
        grid=(B, N // tm),
        in_specs=[pl.BlockSpec((1, N, W), lambda b, i: (b, 0, U_FZ // 256)),
                  pl.BlockSpec((W, 2 * W), lambda b, i: (0, 0)),
                  pl.BlockSpec((tm, N), lambda b, i: (i, 0)),
                  pl.BlockSpec((tm, N), lambda b, i: (i, 0))],
        out_specs=pl.BlockSpec((1, tm, W), lambda b, i: (b, i, 0)),
        out_shape=jax.ShapeDtypeStruct((B, N, W), BF16),
        scratch_shapes=[pltpu.VMEM((N, W), BF16), pltpu.VMEM((N, W), BF16)],
        compiler_params=_cparams("parallel", "arbitrary"),
        name="fourier",
    )(u, cs, cn, sn)


def _win_kernel(*refs, tq, has_local):
    if has_local:
        sink_ref, q_ref, k_ref, v_ref, kc_ref, vc_ref, o_ref = refs
    else:
        sink_ref, q_ref, kc_ref, vc_ref, o_ref = refs
    g, i = pl.program_id(1), pl.program_id(2)
    q = q_ref[0]
    lane = lax.broadcasted_iota(jnp.int32, (1, 2 * WIN_DH), 1)
    lo = lane < WIN_DH
    zero = jnp.zeros_like(q)
    q2 = jnp.concatenate([jnp.where(lo, q, zero), jnp.where(lo, zero, q)], axis=0)
    row = lax.broadcasted_iota(jnp.int32, (2 * tq, 1), 0)
    sink = jnp.where(row < tq, sink_ref[2 * g], sink_ref[2 * g + 1])
    kc, vc = kc_ref[0], vc_ref[0]
    s_c = _dot_nt(q2, kc)
    m = jnp.maximum(jnp.max(s_c, axis=-1, keepdims=True), sink)
    if has_local:
        N = k_ref.shape[1]
        span = tq + 2 * WINDOW
        start = pl.multiple_of(jnp.clip(i * tq - WINDOW, 0, N - span), WINDOW)
        kw = k_ref[0, pl.ds(start, span), :]
        vw = v_ref[0, pl.ds(start, span), :]
        s_l = _dot_nt(q2, kw)
        qpos = i * tq + jnp.where(row < tq, row, row - tq)
        kpos = start + lax.broadcasted_iota(jnp.int32, (1, span), 1)
        s_l = jnp.where(jnp.abs(kpos - qpos) <= WINDOW, s_l, -jnp.inf)
        m = jnp.maximum(m, jnp.max(s_l, axis=-1, keepdims=True))
    e_c = jnp.exp(s_c - m)
    den = jnp.sum(e_c, axis=-1, keepdims=True) + jnp.exp(sink - m)
    if has_local:
        e_l = jnp.exp(s_l - m)
        den = den + jnp.sum(e_l, axis=-1, keepdims=True)
    inv = 1.0 / den
    o2 = _dot((e_c * inv).astype(BF16), vc)
    if has_local:
        o2 = o2 + _dot((e_l * inv).astype(BF16), vw)
    o_ref[0] = jnp.where(lo, o2[:tq], o2[tq:]).astype(BF16)


def _win_call(u, uc, sink, tq):
    has_local = u is not None
    src = u if has_local else uc
    B, N, _ = src.shape
    C = uc.shape[1]
    W = 2 * WIN_DH
    G = WIN_KV_HEADS
    in_specs = [pl.BlockSpec(memory_space=pltpu.SMEM),
                pl.BlockSpec((1, tq, W), lambda b, g, i: (b, i, U_WQ // W + g))]
    args = [sink, src]
    if has_local:
        in_specs += [pl.BlockSpec((1, N, W), lambda b, g, i: (b, 0, U_WK // W + g)),
                     pl.BlockSpec((1, N, W), lambda b, g, i: (b, 0, U_WV // W + g))]
        args += [u, u]
    in_specs += [pl.BlockSpec((1, C, W), lambda b, g, i: (b, 0, U_WK // W + g)),
                 pl.BlockSpec((1, C, W), lambda b, g, i: (b, 0, U_WV // W + g))]
    args += [uc, uc]
    return pl.pallas_call(
        functools.partial(_win_kernel, tq=tq, has_local=has_local),
        grid=(B, G, N // tq),
        in_specs=in_specs,
        out_specs=pl.BlockSpec((1, tq, W), lambda b, g, i: (b, i, g)),
        out_shape=jax.ShapeDtypeStruct((B, N, G * W), BF16),
        compiler_params=_cparams("parallel", "parallel", "parallel"),
        name="win_attn" if has_local else "win_attn_ctx",
    )(*args)


def _route(sel, score):
    EG = EXPERTS_PER_GROUP
    rows = [sel[e:e + 1] for e in range(N_EXPERTS)]
    srow = [score[e:e + 1] for e in range(N_EXPERTS)]
    best, best_score = None, None
    for g in range(N_EXPERT_GROUPS):
        r = rows[EG * g:EG * (g + 1)]
        top2 = None
        for a in range(EG):
            for b in range(a + 1, EG):
                pair = r[a] + r[b]
                top2 = pair if top2 is None else jnp.maximum(top2, pair)
        if g == 0:
            best, best_score = jnp.zeros(top2.shape, jnp.int32), top2
        else:
            upd = top2 > best_score
            best = jnp.where(upd, g, best)
            best_score = jnp.where(upd, top2, best_score)

    def pick(vals, j):
        out = vals[j]
        for g in range(1, N_EXPERT_GROUPS):
            out = jnp.where(best == g, vals[EG * g + j], out)
        return out

    neg = jnp.full(best_score.shape, -jnp.inf, F32)
    m1, m2 = neg, neg
    i1 = i2 = jnp.zeros(best.shape, jnp.int32)
    w1 = w2 = jnp.zeros(best_score.shape, F32)
    for j in range(EG):
        v, sc = pick(rows, j), pick(srow, j)
        gt1 = v > m1
        gt2 = v > m2
        m2 = jnp.where(gt1, m1, jnp.where(gt2, v, m2))
        i2 = jnp.where(gt1, i1, jnp.where(gt2, j, i2))
        w2 = jnp.where(gt1, w1, jnp.where(gt2, sc, w2))
        m1 = jnp.where(gt1, v, m1)
        i1 = jnp.where(gt1, j, i1)
        w1 = jnp.where(gt1, sc, w1)
    tot = w1 + w2
    w1, w2 = w1 / tot, w2 / tot
    e1, e2 = best * EG + i1, best * EG + i2
    erow = lax.broadcasted_iota(jnp.int32, sel.shape, 0)
    return jnp.where(erow == e1, w1, 0.0) + jnp.where(erow == e2, w2, 0.0)


def _outproj_kernel(ya_ref, yb_ref, yc_ref, yd_ref, w_ref, x_ref, g1_ref, sh2_ref, sc2_ref, lg_ref, lb_ref,
                    rw_ref, rb_ref, x1_ref, comb_ref):
    tn = x_ref.shape[1]
    y = _dot(ya_ref[0], w_ref[0:256, :])
    y = y + _dot(yb_ref[0], w_ref[256:512, :])
    y = y + _dot(yc_ref[0], w_ref[512:768, :])
    y = y + _dot(yd_ref[0], w_ref[768:1024, :])
    x1 = _ln(ALPHA * x_ref[0] + g1_ref[0] * y) * lg_ref[...] + lb_ref[...]
    x1_ref[0] = x1
    h2 = _ln(x1) * (1.0 + sc2_ref[0]) + sh2_ref[0]
    logits = lax.dot_general(rw_ref[...], h2, (((1,), (1,)), ((), ())), preferred_element_type=F32,
                             precision=lax.Precision.HIGHEST)
    score = _sigmoid(logits)
    comb = _route(score + rb_ref[...], score)
    comb = jnp.concatenate([comb, jnp.zeros((128 - N_EXPERTS, tn), F32)], axis=0)
    comb_ref[0] = comb.T


def _outproj_call(ys, w_out, x, mod, mod_row, ln_g, ln_b, router_wt, router_bias, tn):
    B, N, D = x.shape
    row = (lambda b: b) if mod_row is None else (lambda b: mod_row)
    yspec = pl.BlockSpec((1, tn, 256), lambda b, i: (b, i, 0))
    mspec = lambda j: pl.BlockSpec((1, 1, D), lambda b, i: (row(b), 0, j))
    vec = pl.BlockSpec((1, D), lambda b, i: (0, 0))
    return pl.pallas_call(
        _outproj_kernel,
        grid=(B, N // tn),
        in_specs=[yspec, yspec, yspec, yspec,
                  pl.BlockSpec((D, D), lambda b, i: (0, 0)),
                  pl.BlockSpec((1, tn, D), lambda b, i: (b, i, 0)),
                  mspec(2), mspec(3), mspec(4), vec, vec,
                  pl.BlockSpec((N_EXPERTS, D), lambda b, i: (0, 0)),
                  pl.BlockSpec((N_EXPERTS, 1), lambda b, i: (0, 0))],
        out_specs=[pl.BlockSpec((1, tn, D), lambda b, i: (b, i, 0)),
                   pl.BlockSpec((1, tn, 128), lambda b, i: (b, i, 0))],
        out_shape=[jax.ShapeDtypeStruct((B, N, D), F32), jax.ShapeDtypeStruct((B, N, 128), F32)],
        compiler_params=_cparams("parallel", "parallel"),
        name="outproj_router",
    )(*ys, w_out, x, mod, mod, mod, ln_g.reshape(1, D), ln_b.reshape(1, D), router_wt, router_bias.reshape(-1, 1))


def _moe_kernel(x1_ref, comb_ref, sh2_ref, sc2_ref, g2_ref, wg_ref, wu_ref, wd_ref, lg_ref, lb_ref, o_ref,
                h_ref, acc_ref):
    e = pl.program_id(1)

    @pl.when(e == 0)
    def _():
        h_ref[...] = (_ln(x1_ref[0]) * (1.0 + sc2_ref[0]) + sh2_ref[0]).astype(BF16)
        acc_ref[...] = jnp.zeros_like(acc_ref)

    h = h_ref[...]
    gate = _dot(h, wg_ref[0])
    up = _dot(h, wu_ref[0])
    act = (gate * _sigmoid(gate) * up).astype(BF16)
    y = _dot(act, wd_ref[0])
    lane = lax.broadcasted_iota(jnp.int32, (1, 128), 1)
    ce = jnp.sum(jnp.where(lane == e, comb_ref[0], 0.0), axis=-1, keepdims=True)
    acc_ref[...] += ce * y

    @pl.when(e == N_EXPERTS - 1)
    def _():
        o_ref[0] = _ln(ALPHA * x1_ref[0] + g2_ref[0] * acc_ref[...]) * lg_ref[...] + lb_ref[...]


def _moe_call(x1, comb, mod, mod_row, wg, wu, wd, ln_g, ln_b, tm):
    B, N, D = x1.shape
    F = wg.shape[-1]
    nt = N // tm
    row = (lambda t: t // nt) if mod_row is None else (lambda t: mod_row)
    mspec = lambda j: pl.BlockSpec((1, 1, D), lambda t, e: (row(t), 0, j))
    vec = pl.BlockSpec((1, D), lambda t, e: (0, 0))
    return pl.pallas_call(
        _moe_kernel,
        grid=(B * nt, N_EXPERTS),
        in_specs=[pl.BlockSpec((1, tm, D), lambda t, e: (t // nt, t % nt, 0)),
                  pl.BlockSpec((1, tm, 128), lambda t, e: (t // nt, t % nt, 0)),
                  mspec(3), mspec(4), mspec(5),
                  pl.BlockSpec((1, D, F), lambda t, e: (e, 0, 0)),
                  pl.BlockSpec((1, D, F), lambda t, e: (e, 0, 0)),
                  pl.BlockSpec((1, F, D), lambda t, e: (e, 0, 0)),
                  vec, vec],
        out_specs=pl.BlockSpec((1, tm, D), lambda t, e: (t // nt, t % nt, 0)),
        out_shape=jax.ShapeDtypeStruct((B, N, D), F32),
        scratch_shapes=[pltpu.VMEM((tm, D), BF16), pltpu.VMEM((tm, D), F32)],
        compiler_params=_cparams("parallel", "arbitrary"),
        name="moe_ffn",
    )(x1, comb, mod, mod, mod, wg, wu, wd, ln_g.reshape(1, D), ln_b.reshape(1, D))


def _pair_swap(n):
    idx = np.arange(n)
    return idx + 1 - 2 * (idx % 2)


def _inproj_weights(w_in_l):
    av, ag, dq, dk, dv, fz, wq, wk, wv = jnp.split(w_in_l, np.cumsum([256, 256, 256, 256, 256, 256, 256, 128])[:8].tolist(), axis=1)
    dup = lambda w: jnp.concatenate([w[:, :WIN_DH], w[:, :WIN_DH], w[:, WIN_DH:], w[:, WIN_DH:]], axis=1)
    wkx, wvx = dup(wk), dup(wv)
    sw = _pair_swap(256)
    cols = [av, ag, dq, dk, dv, fz, wq, wkx, wvx, dq[:, sw], dk[:, sw], wq[:, sw], wkx[:, sw]]
    return jnp.concatenate(cols, axis=1).astype(BF16)


def _rope_tables(N):
    rows = N // GRID_W
    row_pos = jnp.repeat(jnp.arange(rows, dtype=F32), GRID_W)
    col_pos = jnp.tile(jnp.arange(GRID_W, dtype=F32), rows)
    out = []
    for dh in (DIFF_DH, WIN_DH):
        n_axis = dh // 4
        inv = ROPE_BASE ** (-jnp.arange(n_axis, dtype=F32) / n_axis)
        ang = jnp.concatenate([row_pos[:, None] * inv, col_pos[:, None] * inv], -1)
        cos = jnp.repeat(jnp.cos(ang), 2, axis=-1)
        sin = jnp.repeat(jnp.sin(ang), 2, axis=-1) * jnp.tile(jnp.array([-1.0, 1.0], F32), dh // 2)
        out += [jnp.tile(cos, (1, 256 // dh)), jnp.tile(sin, (1, 256 // dh))]
    return tuple(out)


def _pick_tile(n, pref):
    return pref if n % pref == 0 else n


def kernel(x, c, ctx, c_ctx, w_mod, b_mod, w_in, w_out, conv_w, conv_b, conv_norm_g, conv_norm_b, diff_lambda,
           diff_subln_g, win_sink, ln_mix_g, ln_mix_b, ln_ffn_g, ln_ffn_b, router_w, router_bias, exp_w_gate,
           exp_w_up, exp_w_down):
    B, N, D = x.shape
    C = ctx.shape[1]
    ctx_row = B
    pad_rows = (-(B + 1)) % 8
    cs = jnp.concatenate([c, c_ctx[None, :], jnp.zeros((pad_rows, D), F32)], axis=0)
    mod_all = _mod_call(cs, w_mod, b_mod)
    tables = _rope_tables(N)
    dft = {n: _dft_tables(n) for n in {N, C}}
    router_wt = router_w.T
    xc = ctx
    for l in range(DEPTH):
        need_ctx = l < DEPTH - 1
        mod = mod_all[l].reshape(-1, 1, 6 * D)
        lam_init = 0.8 - 0.6 * math.exp(-0.3 * l)
        w_ext = _inproj_weights(w_in[l])
        w_o = w_out[l].astype(BF16)
        wg, wu, wd = exp_w_gate[l].astype(BF16), exp_w_up[l].astype(BF16), exp_w_down[l].astype(BF16)

        u = _inproj_call(x, mod, None, w_ext, tables, _pick_tile(N, 512))
        uc = _inproj_call(xc, mod, ctx_row, w_ext[:, :U_WIDTH], None, _pick_tile(C, 256))

        def mixers(u_lat):
            src = u_lat if u_lat is not None else uc
            n = src.shape[1]
            return (_conv_call(src, conv_w[l], conv_b[l], conv_norm_g[l], conv_norm_b[l]),
                    _diff_call(u_lat, uc, diff_lambda[l], diff_subln_g[l], lam_init, _pick_tile(n, 256)),
                    _fourier_call(src, dft[n], _pick_tile(n, 512)),
                    _win_call(u_lat, uc, win_sink[l], _pick_tile(n, 256)))

        x1, comb = _outproj_call(mixers(u), w_o, x, mod, None, ln_mix_g[l], ln_mix_b[l], router_wt, router_bias,
                                 _pick_tile(N, 512))
        x = _moe_call(x1, comb, mod, None, wg, wu, wd, ln_ffn_g[l], ln_ffn_b[l], _pick_tile(N, 1024))
        if need_ctx:
            xc1, combc = _outproj_call(mixers(None), w_o, xc, mod, ctx_row, ln_mix_g[l], ln_mix_b[l], router_wt,
                                       router_bias, _pick_tile(C, 256))
            xc = _moe_call(xc1, combc, mod, ctx_row, wg, wu, wd, ln_ffn_g[l], ln_ffn_b[l], _pick_tile(C, 256))
    return x
```

```python
import functools
import math

import numpy as np
import jax
import jax.numpy as jnp
from jax import lax
from jax.experimental import pallas as pl
from jax.experimental.pallas import tpu as pltpu

F32 = jnp.float32
BF16 = jnp.bfloat16

D_MODEL = 1024
DEPTH = 2
GRID_W = 64
CONV_CH = 256
CONV_WIDTH = 31
DIFF_HEADS = 4
DIFF_DH = 32
FOURIER_GROUPS = 4
FOURIER_CH = 64
WIN_DH = 64
WIN_KV_HEADS = 2
WINDOW = 128
ROPE_BASE = 10000.0
N_EXPERTS = 16
N_EXPERT_GROUPS = 4
EXPERTS_PER_GROUP = 4
D_EXPERT = 512
ALPHA = (2 * DEPTH) ** 0.25
LN_EPS = 1e-5

U_AV, U_AG, U_DQ, U_DK, U_DV, U_FZ, U_WQ, U_WK, U_WV = (256 * i for i in range(9))
U_WIDTH = 9 * 256
ROPE_GROUPS = (U_DQ, U_DK, U_WQ, U_WK)
V7X_VMEM_LIMIT = 48 * 1024 * 1024


def _cparams(*sem):
    return pltpu.CompilerParams(dimension_semantics=sem, vmem_limit_bytes=V7X_VMEM_LIMIT)


def _ln(x):
    mu = jnp.mean(x, axis=-1, keepdims=True)
    xc = x - mu
    var = jnp.mean(xc * xc, axis=-1, keepdims=True)
    return xc * lax.rsqrt(var + LN_EPS)


def _sigmoid(x):
    return 1.0 / (1.0 + jnp.exp(-x))


def _dot(a, b):
    return jnp.dot(a, b, preferred_element_type=F32)


def _dot_nt(a, b):
    return lax.dot_general(a, b, (((1,), (1,)), ((), ())), preferred_element_type=F32)


def _mod_kernel(c_ref, w_ref, b_ref, o_ref):
    c = c_ref[...]
    s = c * _sigmoid(c)
    o_ref[0] = jnp.dot(s, w_ref[0], preferred_element_type=F32, precision=lax.Precision.HIGHEST) + b_ref[0]


def _mod_call(cs, w_mod, b_mod):
    R, D = cs.shape
    return pl.pallas_call(
        _mod_kernel,
        grid=(DEPTH, 6),
        in_specs=[pl.BlockSpec((R, D), lambda l, j: (0, 0)),
                  pl.BlockSpec((1, D, D), lambda l, j: (l, 0, j)),
                  pl.BlockSpec((1, 1, D), lambda l, j: (l, 0, j))],
        out_specs=pl.BlockSpec((1, R, D), lambda l, j: (l, 0, j)),
        out_shape=jax.ShapeDtypeStruct((DEPTH, R, 6 * D), F32),
        compiler_params=_cparams("arbitrary", "arbitrary"),
        name="mod",
    )(cs, w_mod, b_mod.reshape(DEPTH, 1, 6 * D))


def _inproj_kernel(*refs, rope):
    if rope:
        x_ref, sh_ref, sc_ref, w_ref, cd_ref, sd_ref, cw_ref, sw_ref, o_ref = refs
    else:
        x_ref, sh_ref, sc_ref, w_ref, o_ref = refs
    h = (_ln(x_ref[0]) * (1.0 + sc_ref[0]) + sh_ref[0]).astype(BF16)
    n_sw = 0
    for a in range(0, U_WIDTH, 256):
        val = _dot(h, w_ref[:, a:a + 256])
        if a in (U_DQ, U_WQ):
            val = val * (DIFF_DH ** -0.5 if a == U_DQ else WIN_DH ** -0.5)
        if rope and a in ROPE_GROUPS:
            sw = _dot(h, w_ref[:, U_WIDTH + 256 * n_sw:U_WIDTH + 256 * (n_sw + 1)])
            if a in (U_DQ, U_WQ):
                sw = sw * (DIFF_DH ** -0.5 if a == U_DQ else WIN_DH ** -0.5)
            n_sw += 1
            cos, sin = (cd_ref, sd_ref) if a in (U_DQ, U_DK) else (cw_ref, sw_ref)
            val = val * cos[...] + sw * sin[...]
        o_ref[0, :, a:a + 256] = val.astype(BF16)


def _inproj_call(x, mod, mod_row, w, tables, tn):
    B, N, D = x.shape
    rope = tables is not None
    row = (lambda b: b) if mod_row is None else (lambda b: mod_row)
    in_specs = [pl.BlockSpec((1, tn, D), lambda b, i: (b, i, 0)),
                pl.BlockSpec((1, 1, D), lambda b, i: (row(b), 0, 0)),
                pl.BlockSpec((1, 1, D), lambda b, i: (row(b), 0, 1)),
                pl.BlockSpec(w.shape, lambda b, i: (0, 0))]
    args = [x, mod, mod, w]
    if rope:
        in_specs += [pl.BlockSpec((tn, 256), lambda b, i: (i, 0))] * 4
        args += list(tables)
    return pl.pallas_call(
        functools.partial(_inproj_kernel, rope=rope),
        grid=(B, N // tn),
        in_specs=in_specs,
        out_specs=pl.BlockSpec((1, tn, U_WIDTH), lambda b, i: (b, i, 0)),
        out_shape=jax.ShapeDtypeStruct((B, N, U_WIDTH), BF16),
        compiler_params=_cparams("parallel", "parallel"),
        name="inproj_rope" if rope else "inproj",
    )(*args)


CONV_ROWS = 128
CONV_PAD = 16


def _conv_kernel(u_ref, w_ref, b_ref, g_ref, nb_ref, o_ref, pad_ref, sh_ref):
    N = u_ref.shape[1]
    L = N + 2 * CONV_PAD
    val = u_ref[0, :, 0:CONV_CH].astype(F32)
    gate = u_ref[0, :, CONV_CH:2 * CONV_CH].astype(F32)
    pad_ref[0:CONV_PAD, :] = jnp.zeros((CONV_PAD, CONV_CH), F32)
    pad_ref[CONV_PAD + N:L + 8, :] = jnp.zeros((CONV_PAD + 8, CONV_CH), F32)
    pad_ref[CONV_PAD:CONV_PAD + N, :] = val * _sigmoid(gate)
    for s in range(8):
        sh_ref[s] = pad_ref[s:s + L, :]

    def tile(i, carry):
        r0 = pl.multiple_of(i * CONV_ROWS, CONV_ROWS)
        acc = jnp.zeros((CONV_ROWS, CONV_CH), F32)
        for k in range(CONV_WIDTH):
            off = CONV_PAD - CONV_WIDTH // 2 + k
            acc = acc + w_ref[k:k + 1, :] * sh_ref[off % 8, pl.ds(r0 + 8 * (off // 8), CONV_ROWS), :]
        y = _ln(acc + b_ref[...]) * g_ref[...] + nb_ref[...]
        o_ref[0, pl.ds(r0, CONV_ROWS), :] = (y * _sigmoid(y)).astype(BF16)
        return carry

    lax.fori_loop(0, N // CONV_ROWS, tile, 0)


def _conv_call(u, conv_w, conv_b, conv_ng, conv_nb):
    B, N, _ = u.shape
    vec = pl.BlockSpec((1, CONV_CH), lambda b: (0, 0))
    return pl.pallas_call(
        _conv_kernel,
        grid=(B,),
        in_specs=[pl.BlockSpec((1, N, 2 * CONV_CH), lambda b: (b, 0, 0)),
                  pl.BlockSpec((CONV_WIDTH, CONV_CH), lambda b: (0, 0)), vec, vec, vec],
        out_specs=pl.BlockSpec((1, N, CONV_CH), lambda b: (b, 0, 0)),
        out_shape=jax.ShapeDtypeStruct((B, N, CONV_CH), BF16),
        scratch_shapes=[pltpu.VMEM((N + 2 * CONV_PAD + 8, CONV_CH), F32),
                        pltpu.VMEM((8, N + 2 * CONV_PAD, CONV_CH), F32)],
        compiler_params=_cparams("parallel"),
        name="conv",
    )(u, conv_w, conv_b.reshape(1, -1), conv_ng.reshape(1, -1), conv_nb.reshape(1, -1))


def _diff_kernel(*refs, lam_init, has_lat):
    if has_lat:
        q_ref, k_ref, v_ref, kc_ref, vc_ref, dl_ref, g_ref, o_ref = refs
    else:
        q_ref, kc_ref, vc_ref, dl_ref, g_ref, o_ref = refs
    q = q_ref[0]
    tq = q.shape[0]
    kc, vc = kc_ref[0], vc_ref[0]
    if has_lat:
        k, v = k_ref[0], v_ref[0]
    dl = dl_ref[...]
    lam = (jnp.exp(jnp.sum(dl[0:1] * dl[1:2], axis=-1, keepdims=True))
           - jnp.exp(jnp.sum(dl[2:3] * dl[3:4], axis=-1, keepdims=True)) + lam_init)
    lane = lax.broadcasted_iota(jnp.int32, (1, DIFF_HEADS * 2 * DIFF_DH), 1)
    o = jnp.zeros((tq, DIFF_HEADS * 2 * DIFF_DH), F32)
    head_masks = []
    for h in range(DIFF_HEADS):
        probs = []
        for c in range(2):
            lo = (2 * h + c) * DIFF_DH
            qm = jnp.where((lane >= lo) & (lane < lo + DIFF_DH), q, jnp.zeros_like(q))
            s_c = _dot_nt(qm, kc)
            m = jnp.max(s_c, axis=-1, keepdims=True)
            if has_lat:
                s_l = _dot_nt(qm, k)
                m = jnp.maximum(m, jnp.max(s_l, axis=-1, keepdims=True))
            e_c = jnp.exp(s_c - m)
            den = jnp.sum(e_c, axis=-1, keepdims=True)
            e_l = None
            if has_lat:
                e_l = jnp.exp(s_l - m)
                den = den + jnp.sum(e_l, axis=-1, keepdims=True)
            probs.append((e_c, e_l, 1.0 / den))
        (e_c0, e_l0, i0), (e_c1, e_l1, i1) = probs
        i1 = lam * i1
        oh = _dot((e_c0 * i0 - e_c1 * i1).astype(BF16), vc)
        if has_lat:
            oh = oh + _dot((e_l0 * i0 - e_l1 * i1).astype(BF16), v)
        hm = (lane >= 2 * h * DIFF_DH) & (lane < (2 * h + 2) * DIFF_DH)
        head_masks.append(hm)
        o = o + jnp.where(hm, oh, 0.0)
    o2 = o * o
    ms = jnp.zeros_like(o)
    for hm in head_masks:
        ssq = jnp.sum(jnp.where(hm, o2, 0.0), axis=-1, keepdims=True) * (1.0 / (2 * DIFF_DH))
        ms = ms + jnp.where(hm, ssq, 0.0)
    y = o * lax.rsqrt(ms + LN_EPS) * g_ref[...] * (1.0 - lam_init)
    o_ref[0] = y.astype(BF16)


def _diff_call(u, uc, diff_lambda, subln_g, lam_init, tq):
    has_lat = u is not None
    src = u if has_lat else uc
    B, N, _ = src.shape
    C = uc.shape[1]
    W = DIFF_HEADS * 2 * DIFF_DH
    blk = lambda col: U_DQ // 256 + col
    in_specs = [pl.BlockSpec((1, tq, W), lambda b, i: (b, i, blk(0)))]
    args = [src]
    if has_lat:
        in_specs += [pl.BlockSpec((1, N, W), lambda b, i: (b, 0, blk(1))),
                     pl.BlockSpec((1, N, W), lambda b, i: (b, 0, blk(2)))]
        args += [u, u]
    in_specs += [pl.BlockSpec((1, C, W), lambda b, i: (b, 0, blk(1))),
                 pl.BlockSpec((1, C, W), lambda b, i: (b, 0, blk(2))),
                 pl.BlockSpec((4, DIFF_DH), lambda b, i: (0, 0)),
                 pl.BlockSpec((1, W), lambda b, i: (0, 0))]
    args += [uc, uc, diff_lambda, jnp.tile(subln_g, DIFF_HEADS).reshape(1, W)]
    return pl.pallas_call(
        functools.partial(_diff_kernel, lam_init=lam_init, has_lat=has_lat),
        grid=(B, N // tq),
        in_specs=in_specs,
        out_specs=pl.BlockSpec((1, tq, W), lambda b, i: (b, i, 0)),
        out_shape=jax.ShapeDtypeStruct((B, N, W), BF16),
        compiler_params=_cparams("parallel", "parallel"),
        name="diff_attn" if has_lat else "diff_attn_ctx",
    )(*args)


def _fourier_kernel(z_ref, cs_ref, cn_ref, sn_ref, o_ref, a_ref, b_ref, *, scale):
    W = FOURIER_GROUPS * FOURIER_CH

    @pl.when(pl.program_id(1) == 0)
    def _():
        t = _dot(z_ref[0], cs_ref[...])
        a_ref[...] = t[:, :W].astype(BF16)
        b_ref[...] = t[:, W:].astype(BF16)

    y = _dot(cn_ref[...], a_ref[...]) - _dot(sn_ref[...], b_ref[...])
    o_ref[0] = (y * scale).astype(BF16)


def _dft_tables(N):
    n = jnp.arange(N, dtype=jnp.int32)
    ang = ((n[:, None] * n[None, :]) % N).astype(F32) * (2.0 * math.pi / N)
    c = np.arange(FOURIER_CH, dtype=np.int64)
    angc = 2.0 * np.pi * ((c[:, None] * c[None, :]) % FOURIER_CH).astype(np.float64) / FOURIER_CH
    eye = np.eye(FOURIER_GROUPS)
    cs = np.concatenate([np.kron(eye, np.cos(angc)), np.kron(eye, np.sin(angc))], axis=1)
    return jnp.asarray(cs, F32).astype(BF16), jnp.cos(ang).astype(BF16), jnp.sin(ang).astype(BF16)


def _fourier_call(u, dft, tm):
    B, N, _ = u.shape
    W = FOURIER_GROUPS * FOURIER_CH
    cs, cn, sn = dft
    return pl.pallas_call(
        functools.partial(_fourier_kernel, scale=float((N * FOURIER_CH) ** -0.5)),
        grid=(B, N // tm),
        in_specs=[pl.BlockSpec((1, N, W), lambda b, i: (b, 0, U_FZ // 256)),
                  pl.BlockSpec((W, 2 * W), lambda b, i: (0, 0)),
                  pl.BlockSpec((tm, N), lambda b, i: (i, 0)),
                  pl.BlockSpec((tm, N), lambda b, i: (i, 0))],
        out_specs=pl.BlockSpec((1, tm, W), lambda b, i: (b, i, 0)),
        out_shape=jax.ShapeDtypeStruct((B, N, W), BF16),
        scratch_shapes=[pltpu.VMEM((N, W), BF16), pltpu.VMEM((N, W), BF16)],
        compiler_params=_cparams("parallel", "arbitrary"),
        name="fourier",
    )(u, cs, cn, sn)


def _win_kernel(*refs, tq, has_local):
    if has_local:
        sink_ref, q_ref, k_ref, v_ref, kc_ref, vc_ref, o_ref = refs
    else:
        sink_ref, q_ref, kc_ref, vc_ref, o_ref = refs
    g, i = pl.program_id(1), pl.program_id(2)
    q = q_ref[0]
    lane = lax.broadcasted_iota(jnp.int32, (1, 2 * WIN_DH), 1)
    lo = lane < WIN_DH
    zero = jnp.zeros_like(q)
    q2 = jnp.concatenate([jnp.where(lo, q, zero), jnp.where(lo, zero, q)], axis=0)
    row = lax.broadcasted_iota(jnp.int32, (2 * tq, 1), 0)
    sink = jnp.where(row < tq, sink_ref[2 * g], sink_ref[2 * g + 1])
    kc, vc = kc_ref[0], vc_ref[0]
    s_c = _dot_nt(q2, kc)
    m = jnp.maximum(jnp.max(s_c, axis=-1, keepdims=True), sink)
    if has_local:
        N = k_ref.shape[1]
        span = tq + 2 * WINDOW
        start = pl.multiple_of(jnp.clip(i * tq - WINDOW, 0, N - span), WINDOW)
        kw = k_ref[0, pl.ds(start, span), :]
        vw = v_ref[0, pl.ds(start, span), :]
        s_l = _dot_nt(q2, kw)
        qpos = i * tq + jnp.where(row < tq, row, row - tq)
        kpos = start + lax.broadcasted_iota(jnp.int32, (1, span), 1)
        s_l = jnp.where(jnp.abs(kpos - qpos) <= WINDOW, s_l, -jnp.inf)
        m = jnp.maximum(m, jnp.max(s_l, axis=-1, keepdims=True))
    e_c = jnp.exp(s_c - m)
    den = jnp.sum(e_c, axis=-1, keepdims=True) + jnp.exp(sink - m)
    if has_local:
        e_l = jnp.exp(s_l - m)
        den = den + jnp.sum(e_l, axis=-1, keepdims=True)
    inv = 1.0 / den
    o2 = _dot((e_c * inv).astype(BF16), vc)
    if has_local:
        o2 = o2 + _dot((e_l * inv).astype(BF16), vw)
    o_ref[0] = jnp.where(lo, o2[:tq], o2[tq:]).astype(BF16)


def _win_call(u, uc, sink, tq):
    has_local = u is not None
    src = u if has_local else uc
    B, N, _ = src.shape
    C = uc.shape[1]
    W = 2 * WIN_DH
    G = WIN_KV_HEADS
    in_specs = [pl.BlockSpec(memory_space=pltpu.SMEM),
                pl.BlockSpec((1, tq, W), lambda b, g, i: (b, i, U_WQ // W + g))]
    args = [sink, src]
    if has_local:
        in_specs += [pl.BlockSpec((1, N, W), lambda b, g, i: (b, 0, U_WK // W + g)),
                     pl.BlockSpec((1, N, W), lambda b, g, i: (b, 0, U_WV // W + g))]
        args += [u, u]
    in_specs += [pl.BlockSpec((1, C, W), lambda b, g, i: (b, 0, U_WK // W + g)),
                 pl.BlockSpec((1, C, W), lambda b, g, i: (b, 0, U_WV // W + g))]
    args += [uc, uc]
    return pl.pallas_call(
        functools.partial(_win_kernel, tq=tq, has_local=has_local),
        grid=(B, G, N // tq),
        in_specs=in_specs,
        out_specs=pl.BlockSpec((1, tq, W), lambda b, g, i: (b, i, g)),
        out_shape=jax.ShapeDtypeStruct((B, N, G * W), BF16),
        compiler_params=_cparams("parallel", "parallel", "parallel"),
        name="win_attn" if has_local else "win_attn_ctx",
    )(*args)


def _route(sel, score):
    EG = EXPERTS_PER_GROUP
    rows = [sel[e:e + 1] for e in range(N_EXPERTS)]
    srow = [score[e:e + 1] for e in range(N_EXPERTS)]
    best, best_score = None, None
    for g in range(N_EXPERT_GROUPS):
        r = rows[EG * g:EG * (g + 1)]
        top2 = None
        for a in range(EG):
            for b in range(a + 1, EG):
                pair = r[a] + r[b]
                top2 = pair if top2 is None else jnp.maximum(top2, pair)
        if g == 0:
            best, best_score = jnp.zeros(top2.shape, jnp.int32), top2
        else:
            upd = top2 > best_score
            best = jnp.where(upd, g, best)
            best_score = jnp.where(upd, top2, best_score)

    def pick(vals, j):
        out = vals[j]
        for g in range(1, N_EXPERT_GROUPS):
            out = jnp.where(best == g, vals[EG * g + j], out)
        return out

    neg = jnp.full(best_score.shape, -jnp.inf, F32)
    m1, m2 = neg, neg
    i1 = i2 = jnp.zeros(best.shape, jnp.int32)
    w1 = w2 = jnp.zeros(best_score.shape, F32)
    for j in range(EG):
        v, sc = pick(rows, j), pick(srow, j)
        gt1 = v > m1
        gt2 = v > m2
        m2 = jnp.where(gt1, m1, jnp.where(gt2, v, m2))
        i2 = jnp.where(gt1, i1, jnp.where(gt2, j, i2))
        w2 = jnp.where(gt1, w1, jnp.where(gt2, sc, w2))
        m1 = jnp.where(gt1, v, m1)
        i1 = jnp.where(gt1, j, i1)
        w1 = jnp.where(gt1, sc, w1)
    tot = w1 + w2
    return best * EG + i1, best * EG + i2, w1 / tot, w2 / tot


def _outproj_kernel(ya_ref, yb_ref, yc_ref, yd_ref, w_ref, x_ref, g1_ref, sh2_ref, sc2_ref, lg_ref, lb_ref,
                    rw_ref, rb_ref, x1_ref, h2_ref, wcol_ref, e_ref):
    tn = x_ref.shape[1]
    y = _dot(ya_ref[0], w_ref[0:256, :])
    y = y + _dot(yb_ref[0], w_ref[256:512, :])
    y = y + _dot(yc_ref[0], w_ref[512:768, :])
    y = y + _dot(yd_ref[0], w_ref[768:1024, :])
    x1 = _ln(ALPHA * x_ref[0] + g1_ref[0] * y) * lg_ref[...] + lb_ref[...]
    x1_ref[0] = x1
    h2 = _ln(x1) * (1.0 + sc2_ref[0]) + sh2_ref[0]
    logits = lax.dot_general(rw_ref[...], h2, (((1,), (1,)), ((), ())), preferred_element_type=F32,
                             precision=lax.Precision.HIGHEST)
    h2_ref[0] = h2
    score = _sigmoid(logits)
    e1, e2, w1, w2 = _route(score + rb_ref[...], score)
    e_ref[0] = jnp.concatenate([e1, e2, jnp.zeros((6, tn), jnp.int32)], axis=0)
    wcol_ref[0] = jnp.concatenate([w1, w2, jnp.zeros((126, tn), F32)], axis=0).T


def _outproj_call(ys, w_out, x, mod, mod_row, ln_g, ln_b, router_wt, router_bias, tn):
    B, N, D = x.shape
    row = (lambda b: b) if mod_row is None else (lambda b: mod_row)
    yspec = pl.BlockSpec((1, tn, 256), lambda b, i: (b, i, 0))
    mspec = lambda j: pl.BlockSpec((1, 1, D), lambda b, i: (row(b), 0, j))
    vec = pl.BlockSpec((1, D), lambda b, i: (0, 0))
    return pl.pallas_call(
        _outproj_kernel,
        grid=(B, N // tn),
        in_specs=[yspec, yspec, yspec, yspec,
                  pl.BlockSpec((D, D), lambda b, i: (0, 0)),
                  pl.BlockSpec((1, tn, D), lambda b, i: (b, i, 0)),
                  mspec(2), mspec(3), mspec(4), vec, vec,
                  pl.BlockSpec((N_EXPERTS, D), lambda b, i: (0, 0)),
                  pl.BlockSpec((N_EXPERTS, 1), lambda b, i: (0, 0))],
        out_specs=[pl.BlockSpec((1, tn, D), lambda b, i: (b, i, 0)),
                   pl.BlockSpec((1, tn, D), lambda b, i: (b, i, 0)),
                   pl.BlockSpec((1, tn, 128), lambda b, i: (b, i, 0)),
                   pl.BlockSpec((1, 8, tn), lambda b, i: (b, 0, i))],
        out_shape=[jax.ShapeDtypeStruct((B, N, D), F32), jax.ShapeDtypeStruct((B, N, D), F32),
                   jax.ShapeDtypeStruct((B, N, 128), F32), jax.ShapeDtypeStruct((B, 8, N), jnp.int32)],
        compiler_params=_cparams("parallel", "parallel"),
        name="outproj_router",
    )(*ys, w_out, x, mod, mod, mod, ln_g.reshape(1, D), ln_b.reshape(1, D), router_wt, router_bias.reshape(-1, 1))


MOE_ROWS = 256
MOE_UNROLL = 8


def _sort_tokens(e12, tm):
    B, _, N = e12.shape
    T, E = B * N, N_EXPERTS
    e_flat = jnp.concatenate([e12[:, 0, :].reshape(T), e12[:, 1, :].reshape(T)])
    order = jnp.argsort(e_flat, stable=True).astype(jnp.int32)
    counts = jnp.sum((e_flat[:, None] == jnp.arange(E, dtype=jnp.int32)[None, :]).astype(jnp.int32), axis=0)
    start = jnp.cumsum(counts) - counts
    ptiles = (counts + tm - 1) // tm
    pend = jnp.cumsum(ptiles) * tm
    pstart = pend - ptiles * tm
    n_tiles = jnp.sum(ptiles).astype(jnp.int32)
    n_slots = 2 * T + E * tm
    s = jnp.arange(n_slots, dtype=jnp.int32)
    es = jnp.minimum(jnp.searchsorted(pend, s, side="right").astype(jnp.int32), E - 1)
    r = s - pstart[es]
    valid = (r < counts[es]) & (s < n_tiles * tm)
    pair = order[jnp.clip(start[es] + r, 0, 2 * T - 1)]
    src = jnp.where(valid, pair % T, 0).astype(jnp.int32).reshape(-1, 1, tm)
    dst = jnp.where(valid, pair, 0).astype(jnp.int32).reshape(-1, 1, tm)
    n_valid = jnp.sum(valid.reshape(-1, tm).astype(jnp.int32), axis=1)
    return src, dst, es[::tm], n_tiles.reshape(1), n_valid


def _gffn_kernel(te_ref, nt_ref, nv_ref, src_ref, srcn_ref, dst_ref, dstp_ref, h_hbm, wg_ref, wu_ref, wd_ref,
                 y_hbm, xbuf, ybuf, gsem, ssem):
    del te_ref
    i = pl.program_id(0)
    nt = nt_ref[0]
    slot = i % 2
    tm = ybuf.shape[0]

    def gather_row(idx_ref, buf, r):
        return pltpu.make_async_copy(h_hbm.at[pl.ds(idx_ref[0, 0, r], 1)], xbuf.at[buf, pl.ds(r, 1)], gsem.at[buf])

    def scatter_row(idx_ref, r):
        return pltpu.make_async_copy(ybuf.at[pl.ds(r, 1)], y_hbm.at[pl.ds(idx_ref[0, 0, r], 1)], ssem.at[0])

    def for_rows(n, fn):
        def group(g, carry):
            for k in range(MOE_UNROLL):
                fn(g * MOE_UNROLL + k)
            return carry

        def single(r, carry):
            fn(r)
            return carry

        full = n // MOE_UNROLL
        lax.fori_loop(0, full, group, 0)
        lax.fori_loop(full * MOE_UNROLL, n, single, 0)

    @pl.when(i == 0)
    def _():
        for_rows(tm, lambda r: gather_row(src_ref, 0, r).start())

    @pl.when(i < nt)
    def _():
        @pl.when(i + 1 < nt)
        def _():
            for_rows(tm, lambda r: gather_row(srcn_ref, 1 - slot, r).start())

        for_rows(tm, lambda r: gather_row(src_ref, slot, r).wait())
        x = xbuf[slot].astype(BF16)
        gate = _dot(x, wg_ref[0])
        up = _dot(x, wu_ref[0])
        act = (gate * _sigmoid(gate) * up).astype(BF16)
        y = _dot(act, wd_ref[0])

        @pl.when(i > 0)
        def _():
            for_rows(nv_ref[jnp.maximum(i - 1, 0)], lambda r: scatter_row(dstp_ref, r).wait())

        ybuf[...] = y
        for_rows(nv_ref[i], lambda r: scatter_row(dst_ref, r).start())

        @pl.when(i == nt - 1)
        def _():
            for_rows(nv_ref[i], lambda r: scatter_row(dst_ref, r).wait())


def _gffn_call(h2, meta, wg, wu, wd, tm):
    T, D = h2.shape
    F = wg.shape[-1]
    src, dst, tile_e, n_tiles, n_valid = meta
    n_max = src.shape[0]
    idx = lambda f: pl.BlockSpec((1, 1, tm), lambda i, te, nt, nv: (f(i), 0, 0), memory_space=pltpu.SMEM)
    wspec = lambda shape: pl.BlockSpec(shape, lambda i, te, nt, nv: (te[i], 0, 0))
    return pl.pallas_call(
        _gffn_kernel,
        grid_spec=pltpu.PrefetchScalarGridSpec(
            num_scalar_prefetch=3,
            grid=(n_max,),
            in_specs=[idx(lambda i: i), idx(lambda i: jnp.minimum(i + 1, n_max - 1)),
                      idx(lambda i: i), idx(lambda i: jnp.maximum(i - 1, 0)),
                      pl.BlockSpec(memory_space=pl.ANY),
                      wspec((1, D, F)), wspec((1, D, F)), wspec((1, F, D))],
            out_specs=pl.BlockSpec(memory_space=pl.ANY),
            scratch_shapes=[pltpu.VMEM((2, tm, D), F32), pltpu.VMEM((tm, D), F32),
                            pltpu.SemaphoreType.DMA((2,)), pltpu.SemaphoreType.DMA((1,))]),
        out_shape=jax.ShapeDtypeStruct((2 * T, D), F32),
        compiler_params=_cparams("arbitrary"),
        name="moe_ffn",
    )(tile_e, n_tiles, n_valid, src, src, dst, dst, h2, wg, wu, wd)


def _moe_out_kernel(x1_ref, y0_ref, y1_ref, wcol_ref, g2_ref, lg_ref, lb_ref, o_ref):
    w = wcol_ref[0]
    moe = w[:, 0:1] * y0_ref[...] + w[:, 1:2] * y1_ref[...]
    o_ref[0] = _ln(ALPHA * x1_ref[0] + g2_ref[0] * moe) * lg_ref[...] + lb_ref[...]


def _moe_out_call(x1, ycomb, wcol, mod, mod_row, ln_g, ln_b, tn):
    B, N, D = x1.shape
    nt = N // tn
    row = (lambda b: b) if mod_row is None else (lambda b: mod_row)
    vec = pl.BlockSpec((1, D), lambda b, i: (0, 0))
    return pl.pallas_call(
        _moe_out_kernel,
        grid=(B, nt),
        in_specs=[pl.BlockSpec((1, tn, D), lambda b, i: (b, i, 0)),
                  pl.BlockSpec((tn, D), lambda b, i: (b * nt + i, 0)),
                  pl.BlockSpec((tn, D), lambda b, i: (B * nt + b * nt + i, 0)),
                  pl.BlockSpec((1, tn, 128), lambda b, i: (b, i, 0)),
                  pl.BlockSpec((1, 1, D), lambda b, i: (row(b), 0, 5)),
                  vec, vec],
        out_specs=pl.BlockSpec((1, tn, D), lambda b, i: (b, i, 0)),
        out_shape=jax.ShapeDtypeStruct((B, N, D), F32),
        compiler_params=_cparams("parallel", "parallel"),
        name="moe_out",
    )(x1, ycomb, ycomb, wcol, mod, ln_g.reshape(1, D), ln_b.reshape(1, D))


def _moe_sparse(x1, h2, wcol, e12, mod, mod_row, wg, wu, wd, ln_g, ln_b, tn):
    B, N, D = x1.shape
    ycomb = _gffn_call(h2.reshape(B * N, D), _sort_tokens(e12, MOE_ROWS), wg, wu, wd, MOE_ROWS)
    return _moe_out_call(x1, ycomb, wcol, mod, mod_row, ln_g, ln_b, tn)


def _pair_swap(n):
    idx = np.arange(n)
    return idx + 1 - 2 * (idx % 2)


def _inproj_weights(w_in_l):
    av, ag, dq, dk, dv, fz, wq, wk, wv = jnp.split(w_in_l, np.cumsum([256, 256, 256, 256, 256, 256, 256, 128])[:8].tolist(), axis=1)
    dup = lambda w: jnp.concatenate([w[:, :WIN_DH], w[:, :WIN_DH], w[:, WIN_DH:], w[:, WIN_DH:]], axis=1)
    wkx, wvx = dup(wk), dup(wv)
    sw = _pair_swap(256)
    cols = [av, ag, dq, dk, dv, fz, wq, wkx, wvx, dq[:, sw], dk[:, sw], wq[:, sw], wkx[:, sw]]
    return jnp.concatenate(cols, axis=1).astype(BF16)


def _rope_tables(N):
    rows = N // GRID_W
    row_pos = jnp.repeat(jnp.arange(rows, dtype=F32), GRID_W)
    col_pos = jnp.tile(jnp.arange(GRID_W, dtype=F32), rows)
    out = []
    for dh in (DIFF_DH, WIN_DH):
        n_axis = dh // 4
        inv = ROPE_BASE ** (-jnp.arange(n_axis, dtype=F32) / n_axis)
        ang = jnp.concatenate([row_pos[:, None] * inv, col_pos[:, None] * inv], -1)
        cos = jnp.repeat(jnp.cos(ang), 2, axis=-1)
        sin = jnp.repeat(jnp.sin(ang), 2, axis=-1) * jnp.tile(jnp.array([-1.0, 1.0], F32), dh // 2)
        out += [jnp.tile(cos, (1, 256 // dh)), jnp.tile(sin, (1, 256 // dh))]
    return tuple(out)


def _pick_tile(n, pref):
    return pref if n % pref == 0 else n


def kernel(x, c, ctx, c_ctx, w_mod, b_mod, w_in, w_out, conv_w, conv_b, conv_norm_g, conv_norm_b, diff_lambda,
           diff_subln_g, win_sink, ln_mix_g, ln_mix_b, ln_ffn_g, ln_ffn_b, router_w, router_bias, exp_w_gate,
           exp_w_up, exp_w_down):
    B, N, D = x.shape
    C = ctx.shape[1]
    ctx_row = B
    pad_rows = (-(B + 1)) % 8
    cs = jnp.concatenate([c, c_ctx[None, :], jnp.zeros((pad_rows, D), F32)], axis=0)
    mod_all = _mod_call(cs, w_mod, b_mod)
    tables = _rope_tables(N)
    dft = {n: _dft_tables(n) for n in {N, C}}
    router_wt = router_w.T
    xc = ctx
    for l in range(DEPTH):
        need_ctx = l < DEPTH - 1
        mod = mod_all[l].reshape(-1, 1, 6 * D)
        lam_init = 0.8 - 0.6 * math.exp(-0.3 * l)
        w_ext = _inproj_weights(w_in[l])
        w_o = w_out[l].astype(BF16)
        wg, wu, wd = exp_w_gate[l].astype(BF16), exp_w_up[l].astype(BF16), exp_w_down[l].astype(BF16)

        u = _inproj_call(x, mod, None, w_ext, tables, _pick_tile(N, 512))
        uc = _inproj_call(xc, mod, ctx_row, w_ext[:, :U_WIDTH], None, _pick_tile(C, 256))

        def mixers(u_lat):
            src = u_lat if u_lat is not None else uc
            n = src.shape[1]
            return (_conv_call(src, conv_w[l], conv_b[l], conv_norm_g[l], conv_norm_b[l]),
                    _diff_call(u_lat, uc, diff_lambda[l], diff_subln_g[l], lam_init, _pick_tile(n, 256)),
                    _fourier_call(src, dft[n], _pick_tile(n, 512)),
                    _win_call(u_lat, uc, win_sink[l], _pick_tile(n, 256)))

        routed = _outproj_call(mixers(u), w_o, x, mod, None, ln_mix_g[l], ln_mix_b[l], router_wt, router_bias,
                               _pick_tile(N, 512))
        x = _moe_sparse(*routed, mod, None, wg, wu, wd, ln_ffn_g[l], ln_ffn_b[l], _pick_tile(N, 512))
        if need_ctx:
            routed = _outproj_call(mixers(None), w_o, xc, mod, ctx_row, ln_mix_g[l], ln_mix_b[l], router_wt,
                                   router_bias, _pick_tile(C, 256))
            xc = _moe_sparse(*routed, mod, ctx_row, wg, wu, wd, ln_ffn_g[l], ln_ffn_b[l], _pick_tile(C, 256))
    return x
```

```python
import functools
import math

import numpy as np
import jax
import jax.numpy as jnp
from jax import lax
from jax.experimental import pallas as pl
from jax.experimental.pallas import tpu as pltpu

F32 = jnp.float32
BF16 = jnp.bfloat16

D_MODEL = 1024
DEPTH = 2
GRID_W = 64
CONV_CH = 256
CONV_WIDTH = 31
DIFF_HEADS = 4
DIFF_DH = 32
FOURIER_GROUPS = 4
FOURIER_CH = 64
WIN_DH = 64
WIN_KV_HEADS = 2
WINDOW = 128
ROPE_BASE = 10000.0
N_EXPERTS = 16
N_EXPERT_GROUPS = 4
EXPERTS_PER_GROUP = 4
D_EXPERT = 512
ALPHA = (2 * DEPTH) ** 0.25
LN_EPS = 1e-5

U_AV, U_AG, U_DQ, U_DK, U_DV, U_FZ, U_WQ, U_WK, U_WV = (256 * i for i in range(9))
U_WIDTH = 9 * 256
ROPE_GROUPS = (U_DQ, U_DK, U_WQ, U_WK)
V7X_VMEM_LIMIT = 48 * 1024 * 1024


def _cparams(*sem):
    return pltpu.CompilerParams(dimension_semantics=sem, vmem_limit_bytes=V7X_VMEM_LIMIT)


def _ln(x):
    mu = jnp.mean(x, axis=-1, keepdims=True)
    xc = x - mu
    var = jnp.mean(xc * xc, axis=-1, keepdims=True)
    return xc * lax.rsqrt(var + LN_EPS)


def _sigmoid(x):
    return 1.0 / (1.0 + jnp.exp(-x))


def _dot(a, b):
    return jnp.dot(a, b, preferred_element_type=F32)


def _dot_nt(a, b):
    return lax.dot_general(a, b, (((1,), (1,)), ((), ())), preferred_element_type=F32)


def _mod_kernel(c_ref, w_ref, b_ref, o_ref):
    c = c_ref[...]
    s = c * _sigmoid(c)
    o_ref[0] = jnp.dot(s, w_ref[0], preferred_element_type=F32, precision=lax.Precision.HIGHEST) + b_ref[0]


def _mod_call(cs, w_mod, b_mod):
    R, D = cs.shape
    return pl.pallas_call(
        _mod_kernel,
        grid=(DEPTH, 6),
        in_specs=[pl.BlockSpec((R, D), lambda l, j: (0, 0)),
                  pl.BlockSpec((1, D, D), lambda l, j: (l, 0, j)),
                  pl.BlockSpec((1, 1, D), lambda l, j: (l, 0, j))],
        out_specs=pl.BlockSpec((1, R, D), lambda l, j: (l, 0, j)),
        out_shape=jax.ShapeDtypeStruct((DEPTH, R, 6 * D), F32),
        compiler_params=_cparams("arbitrary", "arbitrary"),
        name="mod",
    )(cs, w_mod, b_mod.reshape(DEPTH, 1, 6 * D))


def _inproj_kernel(*refs, rope):
    if rope:
        x_ref, sh_ref, sc_ref, w_ref, cd_ref, sd_ref, cw_ref, sw_ref, o_ref = refs
    else:
        x_ref, sh_ref, sc_ref, w_ref, o_ref = refs
    h = (_ln(x_ref[0]) * (1.0 + sc_ref[0]) + sh_ref[0]).astype(BF16)
    q_scale = {U_DQ: DIFF_DH ** -0.5 * math.log2(math.e), U_WQ: WIN_DH ** -0.5}
    n_sw = 0
    for a in range(0, U_WIDTH, 256):
        val = _dot(h, w_ref[:, a:a + 256])
        if a in q_scale:
            val = val * q_scale[a]
        if rope and a in ROPE_GROUPS:
            sw = _dot(h, w_ref[:, U_WIDTH + 256 * n_sw:U_WIDTH + 256 * (n_sw + 1)])
            if a in q_scale:
                sw = sw * q_scale[a]
            n_sw += 1
            cos, sin = (cd_ref, sd_ref) if a in (U_DQ, U_DK) else (cw_ref, sw_ref)
            val = val * cos[...] + sw * sin[...]
        o_ref[0, :, a:a + 256] = val.astype(BF16)


def _inproj_call(x, mod, mod_row, w, tables, tn):
    B, N, D = x.shape
    rope = tables is not None
    row = (lambda b: b) if mod_row is None else (lambda b: mod_row)
    in_specs = [pl.BlockSpec((1, tn, D), lambda b, i: (b, i, 0)),
                pl.BlockSpec((1, 1, D), lambda b, i: (row(b), 0, 0)),
                pl.BlockSpec((1, 1, D), lambda b, i: (row(b), 0, 1)),
                pl.BlockSpec(w.shape, lambda b, i: (0, 0))]
    args = [x, mod, mod, w]
    if rope:
        in_specs += [pl.BlockSpec((tn, 256), lambda b, i: (i, 0))] * 4
        args += list(tables)
    return pl.pallas_call(
        functools.partial(_inproj_kernel, rope=rope),
        grid=(B, N // tn),
        in_specs=in_specs,
        out_specs=pl.BlockSpec((1, tn, U_WIDTH), lambda b, i: (b, i, 0)),
        out_shape=jax.ShapeDtypeStruct((B, N, U_WIDTH), BF16),
        compiler_params=_cparams("parallel", "parallel"),
        name="inproj_rope" if rope else "inproj",
    )(*args)


CONV_ROWS = 128
CONV_PAD = 16


def _conv_kernel(u_ref, w_ref, b_ref, g_ref, nb_ref, o_ref, pad_ref, sh_ref):
    N = u_ref.shape[1]
    L = N + 2 * CONV_PAD
    val = u_ref[0, :, 0:CONV_CH].astype(F32)
    gate = u_ref[0, :, CONV_CH:2 * CONV_CH].astype(F32)
    pad_ref[0:CONV_PAD, :] = jnp.zeros((CONV_PAD, CONV_CH), F32)
    pad_ref[CONV_PAD + N:L + 8, :] = jnp.zeros((CONV_PAD + 8, CONV_CH), F32)
    pad_ref[CONV_PAD:CONV_PAD + N, :] = val * _sigmoid(gate)
    for s in range(8):
        sh_ref[s] = pad_ref[s:s + L, :]

    def tile(i, carry):
        r0 = pl.multiple_of(i * CONV_ROWS, CONV_ROWS)
        acc = jnp.zeros((CONV_ROWS, CONV_CH), F32)
        for k in range(CONV_WIDTH):
            off = CONV_PAD - CONV_WIDTH // 2 + k
            acc = acc + w_ref[k:k + 1, :] * sh_ref[off % 8, pl.ds(r0 + 8 * (off // 8), CONV_ROWS), :]
        y = _ln(acc + b_ref[...]) * g_ref[...] + nb_ref[...]
        o_ref[0, pl.ds(r0, CONV_ROWS), :] = (y * _sigmoid(y)).astype(BF16)
        return carry

    lax.fori_loop(0, N // CONV_ROWS, tile, 0)


def _conv_call(u, conv_w, conv_b, conv_ng, conv_nb):
    B, N, _ = u.shape
    vec = pl.BlockSpec((1, CONV_CH), lambda b: (0, 0))
    return pl.pallas_call(
        _conv_kernel,
        grid=(B,),
        in_specs=[pl.BlockSpec((1, N, 2 * CONV_CH), lambda b: (b, 0, 0)),
                  pl.BlockSpec((CONV_WIDTH, CONV_CH), lambda b: (0, 0)), vec, vec, vec],
        out_specs=pl.BlockSpec((1, N, CONV_CH), lambda b: (b, 0, 0)),
        out_shape=jax.ShapeDtypeStruct((B, N, CONV_CH), BF16),
        scratch_shapes=[pltpu.VMEM((N + 2 * CONV_PAD + 8, CONV_CH), F32),
                        pltpu.VMEM((8, N + 2 * CONV_PAD, CONV_CH), F32)],
        compiler_params=_cparams("parallel"),
        name="conv",
    )(u, conv_w, conv_b.reshape(1, -1), conv_ng.reshape(1, -1), conv_nb.reshape(1, -1))


def _diff_kernel(*refs, lam_init, has_lat):
    if has_lat:
        q_ref, k_ref, v_ref, kc_ref, vc_ref, dl_ref, g_ref, o_ref, kall_ref, vt_ref = refs
    else:
        q_ref, kc_ref, vc_ref, dl_ref, g_ref, o_ref, kall_ref, vt_ref = refs
    C = kc_ref.shape[1]
    DV = 2 * DIFF_DH

    @pl.when(pl.program_id(1) == 0)
    def _():
        kall_ref[0:C, :] = kc_ref[0]
        vt_ref[:, 0:C] = vc_ref[0].astype(F32).T.astype(BF16)
        if has_lat:
            N = k_ref.shape[1]
            kall_ref[C:C + N, :] = k_ref[0]
            vt_ref[:, C:C + N] = v_ref[0].astype(F32).T.astype(BF16)

    q = q_ref[0]
    kall = kall_ref[...]
    dl = dl_ref[...]
    lam = (jnp.exp(jnp.sum(dl[0:1] * dl[1:2], axis=-1, keepdims=True))
           - jnp.exp(jnp.sum(dl[2:3] * dl[3:4], axis=-1, keepdims=True)) + lam_init)
    lane = lax.broadcasted_iota(jnp.int32, (1, DIFF_HEADS * DV), 1)
    outs = []
    for h in range(DIFF_HEADS):
        parts = []
        for c in range(2):
            lo = (2 * h + c) * DIFF_DH
            km = jnp.where((lane >= lo) & (lane < lo + DIFF_DH), kall, jnp.zeros_like(kall))
            st = _dot_nt(km, q)
            e = jnp.exp2(st - jnp.max(st, axis=0, keepdims=True))
            parts.append((e, jnp.sum(e, axis=0, keepdims=True)))
        (e0, l0), (e1, l1) = parts
        at = (e0 - (lam * l0 / l1) * e1).astype(BF16)
        ot = _dot(vt_ref[DV * h:DV * (h + 1), :], at) * (1.0 / l0)
        ms = jnp.mean(ot * ot, axis=0, keepdims=True)
        outs.append(ot * lax.rsqrt(ms + LN_EPS))
    yt = jnp.concatenate(outs, axis=0) * g_ref[...] * (1.0 - lam_init)
    o_ref[0] = yt.T.astype(BF16)


def _diff_call(u, uc, diff_lambda, subln_g, lam_init, tq):
    has_lat = u is not None
    src = u if has_lat else uc
    B, N, _ = src.shape
    C = uc.shape[1]
    W = DIFF_HEADS * 2 * DIFF_DH
    n_keys = C + N if has_lat else C
    blk = lambda col: U_DQ // 256 + col
    in_specs = [pl.BlockSpec((1, tq, W), lambda b, i: (b, i, blk(0)))]
    args = [src]
    if has_lat:
        in_specs += [pl.BlockSpec((1, N, W), lambda b, i: (b, 0, blk(1))),
                     pl.BlockSpec((1, N, W), lambda b, i: (b, 0, blk(2)))]
        args += [u, u]
    in_specs += [pl.BlockSpec((1, C, W), lambda b, i: (b, 0, blk(1))),
                 pl.BlockSpec((1, C, W), lambda b, i: (b, 0, blk(2))),
                 pl.BlockSpec((4, DIFF_DH), lambda b, i: (0, 0)),
                 pl.BlockSpec((W, 1), lambda b, i: (0, 0))]
    args += [uc, uc, diff_lambda, jnp.tile(subln_g, DIFF_HEADS).reshape(W, 1)]
    return pl.pallas_call(
        functools.partial(_diff_kernel, lam_init=lam_init, has_lat=has_lat),
        grid=(B, N // tq),
        in_specs=in_specs,
        out_specs=pl.BlockSpec((1, tq, W), lambda b, i: (b, i, 0)),
        out_shape=jax.ShapeDtypeStruct((B, N, W), BF16),
        scratch_shapes=[pltpu.VMEM((n_keys, W), BF16), pltpu.VMEM((W, n_keys), BF16)],
        compiler_params=_cparams("parallel", "arbitrary"),
        name="diff_attn" if has_lat else "diff_attn_ctx",
    )(*args)


def _fourier_kernel(z_ref, cs_ref, cn_ref, sn_ref, o_ref, a_ref, b_ref, *, scale):
    W = FOURIER_GROUPS * FOURIER_CH

    @pl.when(pl.program_id(1) == 0)
    def _():
        t = _dot(z_ref[0], cs_ref[...])
        a_ref[...] = t[:, :W].astype(BF16)
        b_ref[...] = t[:, W:].astype(BF16)

    y = _dot(cn_ref[...], a_ref[...]) - _dot(sn_ref[...], b_ref[...])
    o_ref[0] = (y * scale).astype(BF16)


def _dft_tables(N):
    n = jnp.arange(N, dtype=jnp.int32)
    ang = ((n[:, None] * n[None, :]) % N).astype(F32) * (2.0 * math.pi / N)
    c = np.arange(FOURIER_CH, dtype=np.int64)
    angc = 2.0 * np.pi * ((c[:, None] * c[None, :]) % FOURIER_CH).astype(np.float64) / FOURIER_CH
    eye = np.eye(FOURIER_GROUPS)
    cs = np.concatenate([np.kron(eye, np.cos(angc)), np.kron(eye, np.sin(angc))], axis=1)
    return jnp.asarray(cs, F32).astype(BF16), jnp.cos(ang).astype(BF16), jnp.sin(ang).astype(BF16)


def _fourier_call(u, dft, tm):
    B, N, _ = u.shape
    W = FOURIER_GROUPS * FOURIER_CH
    cs, cn, sn = dft
    return pl.pallas_call(
        functools.partial(_fourier_kernel, scale=float((N * FOURIER_CH) ** -0.5)),
        grid=(B, N // tm),
        in_specs=[pl.BlockSpec((1, N, W), lambda b, i: (b, 0, U_FZ // 256)),
                  pl.BlockSpec((W, 2 * W), lambda b, i: (0, 0)),
                  pl.BlockSpec((tm, N), lambda b, i: (i, 0)),
                  pl.BlockSpec((tm, N), lambda b, i: (i, 0))],
        out_specs=pl.BlockSpec((1, tm, W), lambda b, i: (b, i, 0)),
        out_shape=jax.ShapeDtypeStruct((B, N, W), BF16),
        scratch_shapes=[pltpu.VMEM((N, W), BF16), pltpu.VMEM((N, W), BF16)],
        compiler_params=_cparams("parallel", "arbitrary"),
        name="fourier",
    )(u, cs, cn, sn)


def _win_kernel(*refs, tq, has_local):
    if has_local:
        sink_ref, q_ref, k_ref, v_ref, kc_ref, vc_ref, o_ref = refs
    else:
        sink_ref, q_ref, kc_ref, vc_ref, o_ref = refs
    g, i = pl.program_id(1), pl.program_id(2)
    q = q_ref[0]
    lane = lax.broadcasted_iota(jnp.int32, (1, 2 * WIN_DH), 1)
    lo = lane < WIN_DH
    zero = jnp.zeros_like(q)
    q2 = jnp.concatenate([jnp.where(lo, q, zero), jnp.where(lo, zero, q)], axis=0)
    row = lax.broadcasted_iota(jnp.int32, (2 * tq, 1), 0)
    sink = jnp.where(row < tq, sink_ref[2 * g], sink_ref[2 * g + 1])
    kc, vc = kc_ref[0], vc_ref[0]
    s_c = _dot_nt(q2, kc)
    m = jnp.maximum(jnp.max(s_c, axis=-1, keepdims=True), sink)
    if has_local:
        N = k_ref.shape[1]
        span = tq + 2 * WINDOW
        start = pl.multiple_of(jnp.clip(i * tq - WINDOW, 0, N - span), WINDOW)
        kw = k_ref[0, pl.ds(start, span), :]
        vw = v_ref[0, pl.ds(start, span), :]
        s_l = _dot_nt(q2, kw)
        qpos = i * tq + jnp.where(row < tq, row, row - tq)
        kpos = start + lax.broadcasted_iota(jnp.int32, (1, span), 1)
        s_l = jnp.where(jnp.abs(kpos - qpos) <= WINDOW, s_l, -jnp.inf)
        m = jnp.maximum(m, jnp.max(s_l, axis=-1, keepdims=True))
    e_c = jnp.exp(s_c - m)
    den = jnp.sum(e_c, axis=-1, keepdims=True) + jnp.exp(sink - m)
    if has_local:
        e_l = jnp.exp(s_l - m)
        den = den + jnp.sum(e_l, axis=-1, keepdims=True)
    inv = 1.0 / den
    o2 = _dot((e_c * inv).astype(BF16), vc)
    if has_local:
        o2 = o2 + _dot((e_l * inv).astype(BF16), vw)
    o_ref[0] = jnp.where(lo, o2[:tq], o2[tq:]).astype(BF16)


def _win_call(u, uc, sink, tq):
    has_local = u is not None
    src = u if has_local else uc
    B, N, _ = src.shape
    C = uc.shape[1]
    W = 2 * WIN_DH
    G = WIN_KV_HEADS
    in_specs = [pl.BlockSpec(memory_space=pltpu.SMEM),
                pl.BlockSpec((1, tq, W), lambda b, g, i: (b, i, U_WQ // W + g))]
    args = [sink, src]
    if has_local:
        in_specs += [pl.BlockSpec((1, N, W), lambda b, g, i: (b, 0, U_WK // W + g)),
                     pl.BlockSpec((1, N, W), lambda b, g, i: (b, 0, U_WV // W + g))]
        args += [u, u]
    in_specs += [pl.BlockSpec((1, C, W), lambda b, g, i: (b, 0, U_WK // W + g)),
                 pl.BlockSpec((1, C, W), lambda b, g, i: (b, 0, U_WV // W + g))]
    args += [uc, uc]
    return pl.pallas_call(
        functools.partial(_win_kernel, tq=tq, has_local=has_local),
        grid=(B, G, N // tq),
        in_specs=in_specs,
        out_specs=pl.BlockSpec((1, tq, W), lambda b, g, i: (b, i, g)),
        out_shape=jax.ShapeDtypeStruct((B, N, G * W), BF16),
        compiler_params=_cparams("parallel", "parallel", "parallel"),
        name="win_attn" if has_local else "win_attn_ctx",
    )(*args)


def _route(sel, score):
    EG = EXPERTS_PER_GROUP
    rows = [sel[e:e + 1] for e in range(N_EXPERTS)]
    srow = [score[e:e + 1] for e in range(N_EXPERTS)]
    best, best_score = None, None
    for g in range(N_EXPERT_GROUPS):
        r = rows[EG * g:EG * (g + 1)]
        top2 = None
        for a in range(EG):
            for b in range(a + 1, EG):
                pair = r[a] + r[b]
                top2 = pair if top2 is None else jnp.maximum(top2, pair)
        if g == 0:
            best, best_score = jnp.zeros(top2.shape, jnp.int32), top2
        else:
            upd = top2 > best_score
            best = jnp.where(upd, g, best)
            best_score = jnp.where(upd, top2, best_score)

    def pick(vals, j):
        out = vals[j]
        for g in range(1, N_EXPERT_GROUPS):
            out = jnp.where(best == g, vals[EG * g + j], out)
        return out

    neg = jnp.full(best_score.shape, -jnp.inf, F32)
    m1, m2 = neg, neg
    i1 = i2 = jnp.zeros(best.shape, jnp.int32)
    w1 = w2 = jnp.zeros(best_score.shape, F32)
    for j in range(EG):
        v, sc = pick(rows, j), pick(srow, j)
        gt1 = v > m1
        gt2 = v > m2
        m2 = jnp.where(gt1, m1, jnp.where(gt2, v, m2))
        i2 = jnp.where(gt1, i1, jnp.where(gt2, j, i2))
        w2 = jnp.where(gt1, w1, jnp.where(gt2, sc, w2))
        m1 = jnp.where(gt1, v, m1)
        i1 = jnp.where(gt1, j, i1)
        w1 = jnp.where(gt1, sc, w1)
    tot = w1 + w2
    return best * EG + i1, best * EG + i2, w1 / tot, w2 / tot


def _outproj_kernel(ya_ref, yb_ref, yc_ref, yd_ref, w_ref, x_ref, g1_ref, sh2_ref, sc2_ref, lg_ref, lb_ref,
                    rw_ref, rb_ref, x1_ref, h2_ref, wcol_ref, e_ref):
    tn = x_ref.shape[1]
    y = _dot(ya_ref[0], w_ref[0:256, :])
    y = y + _dot(yb_ref[0], w_ref[256:512, :])
    y = y + _dot(yc_ref[0], w_ref[512:768, :])
    y = y + _dot(yd_ref[0], w_ref[768:1024, :])
    x1 = _ln(ALPHA * x_ref[0] + g1_ref[0] * y) * lg_ref[...] + lb_ref[...]
    x1_ref[0] = x1
    h2 = _ln(x1) * (1.0 + sc2_ref[0]) + sh2_ref[0]
    logits = lax.dot_general(rw_ref[...], h2, (((1,), (1,)), ((), ())), preferred_element_type=F32,
                             precision=lax.Precision.HIGHEST)
    h2_ref[0] = h2
    score = _sigmoid(logits)
    e1, e2, w1, w2 = _route(score + rb_ref[...], score)
    e_ref[0] = jnp.concatenate([e1, e2, jnp.zeros((6, tn), jnp.int32)], axis=0)
    wcol_ref[0] = jnp.concatenate([w1, w2, jnp.zeros((126, tn), F32)], axis=0).T


def _outproj_call(ys, w_out, x, mod, mod_row, ln_g, ln_b, router_wt, router_bias, tn):
    B, N, D = x.shape
    row = (lambda b: b) if mod_row is None else (lambda b: mod_row)
    yspec = pl.BlockSpec((1, tn, 256), lambda b, i: (b, i, 0))
    mspec = lambda j: pl.BlockSpec((1, 1, D), lambda b, i: (row(b), 0, j))
    vec = pl.BlockSpec((1, D), lambda b, i: (0, 0))
    return pl.pallas_call(
        _outproj_kernel,
        grid=(B, N // tn),
        in_specs=[yspec, yspec, yspec, yspec,
                  pl.BlockSpec((D, D), lambda b, i: (0, 0)),
                  pl.BlockSpec((1, tn, D), lambda b, i: (b, i, 0)),
                  mspec(2), mspec(3), mspec(4), vec, vec,
                  pl.BlockSpec((N_EXPERTS, D), lambda b, i: (0, 0)),
                  pl.BlockSpec((N_EXPERTS, 1), lambda b, i: (0, 0))],
        out_specs=[pl.BlockSpec((1, tn, D), lambda b, i: (b, i, 0)),
                   pl.BlockSpec((1, tn, D), lambda b, i: (b, i, 0)),
                   pl.BlockSpec((1, tn, 128), lambda b, i: (b, i, 0)),
                   pl.BlockSpec((1, 8, tn), lambda b, i: (b, 0, i))],
        out_shape=[jax.ShapeDtypeStruct((B, N, D), F32), jax.ShapeDtypeStruct((B, N, D), F32),
                   jax.ShapeDtypeStruct((B, N, 128), F32), jax.ShapeDtypeStruct((B, 8, N), jnp.int32)],
        compiler_params=_cparams("parallel", "parallel"),
        name="outproj_router",
    )(*ys, w_out, x, mod, mod, mod, ln_g.reshape(1, D), ln_b.reshape(1, D), router_wt, router_bias.reshape(-1, 1))


MOE_ROWS = 512
MOE_UNROLL = 8


def _sort_tokens(e12, tm):
    B, _, N = e12.shape
    T, E = B * N, N_EXPERTS
    e_flat = jnp.concatenate([e12[:, 0, :].reshape(T), e12[:, 1, :].reshape(T)])
    onehot = (e_flat[:, None] == jnp.arange(E, dtype=jnp.int32)[None, :]).astype(jnp.int32)
    csum = jnp.cumsum(onehot, axis=0)
    counts = csum[-1]
    ptiles = (counts + tm - 1) // tm
    pend = jnp.cumsum(ptiles) * tm
    pstart = pend - ptiles * tm
    n_tiles = jnp.sum(ptiles).astype(jnp.int32)
    slot = jnp.sum(onehot * (pstart[None, :] + csum), axis=1) - 1
    pair_of_slot = jnp.zeros((2 * T + E * tm,), jnp.int32).at[slot].set(
        jnp.arange(2 * T, dtype=jnp.int32), unique_indices=True)
    tile0 = jnp.arange((2 * T + E * tm) // tm, dtype=jnp.int32) * tm
    tile_e = jnp.minimum(jnp.sum((pend[None, :] <= tile0[:, None]).astype(jnp.int32), axis=1), E - 1)
    n_valid = jnp.where(tile0 < n_tiles * tm, jnp.clip(counts[tile_e] - (tile0 - pstart[tile_e]), 0, tm), 0)
    return ((pair_of_slot % T).reshape(-1, 1, tm), pair_of_slot.reshape(-1, 1, tm), tile_e, n_tiles.reshape(1),
            n_valid.astype(jnp.int32))


def _gffn_kernel(te_ref, nt_ref, nv_ref, src_ref, srcn_ref, dst_ref, h_hbm, wg_ref, wu_ref, wd_ref,
                 y_hbm, xbuf, ybuf, gsem, ssem):
    del te_ref
    i = pl.program_id(0)
    nt = nt_ref[0]
    slot = i % 2
    tm = ybuf.shape[0]

    def gather_row(idx_ref, buf, r):
        return pltpu.make_async_copy(h_hbm.at[pl.ds(idx_ref[0, 0, r], 1)], xbuf.at[buf, pl.ds(r, 1)], gsem.at[buf])

    def gather_wait(buf):
        pltpu.make_async_copy(h_hbm.at[pl.ds(0, tm)], xbuf.at[buf], gsem.at[buf]).wait()

    def scatter_row(r):
        return pltpu.make_async_copy(ybuf.at[pl.ds(r, 1)], y_hbm.at[pl.ds(dst_ref[0, 0, r], 1)], ssem.at[0])

    def scatter_wait(n):
        n8 = pl.multiple_of((n // 8) * 8, 8)

        @pl.when(n8 > 0)
        def _():
            pltpu.make_async_copy(ybuf.at[pl.ds(0, n8)], y_hbm.at[pl.ds(0, n8)], ssem.at[0]).wait()

        def single(r, carry):
            scatter_row(r).wait()
            return carry

        lax.fori_loop(n8, n, single, 0)

    def for_rows(n, fn):
        def group(g, carry):
            for k in range(MOE_UNROLL):
                fn(g * MOE_UNROLL + k)
            return carry

        def single(r, carry):
            fn(r)
            return carry

        full = n // MOE_UNROLL
        lax.fori_loop(0, full, group, 0)
        lax.fori_loop(full * MOE_UNROLL, n, single, 0)

    @pl.when(i == 0)
    def _():
        for_rows(tm, lambda r: gather_row(src_ref, 0, r).start())

    @pl.when(i < nt)
    def _():
        gather_wait(slot)
        for r in range(tm):
            gather_row(srcn_ref, 1 - slot, r).start()
        x = xbuf[slot].astype(BF16)
        gate = _dot(x, wg_ref[0])
        up = _dot(x, wu_ref[0])
        act = (gate * _sigmoid(gate) * up).astype(BF16)
        y = _dot(act, wd_ref[0])

        @pl.when(i > 0)
        def _():
            scatter_wait(nv_ref[jnp.maximum(i - 1, 0)])

        ybuf[...] = y
        for_rows(nv_ref[i], lambda r: scatter_row(r).start())

        @pl.when(i == nt - 1)
        def _():
            scatter_wait(nv_ref[i])
            gather_wait(1 - slot)


def _gffn_call(h2, meta, wg, wu, wd, tm):
    T, D = h2.shape
    F = wg.shape[-1]
    src, dst, tile_e, n_tiles, n_valid = meta
    n_max = src.shape[0]
    idx = lambda f: pl.BlockSpec((1, 1, tm), lambda i, te, nt, nv: (f(i, nt), 0, 0), memory_space=pltpu.SMEM)
    wspec = lambda shape: pl.BlockSpec(shape, lambda i, te, nt, nv: (te[i], 0, 0))
    return pl.pallas_call(
        _gffn_kernel,
        grid_spec=pltpu.PrefetchScalarGridSpec(
            num_scalar_prefetch=3,
            grid=(n_max,),
            in_specs=[idx(lambda i, nt: i), idx(lambda i, nt: jnp.minimum(i + 1, nt[0] - 1)),
                      idx(lambda i, nt: i),
                      pl.BlockSpec(memory_space=pl.ANY),
                      wspec((1, D, F)), wspec((1, D, F)), wspec((1, F, D))],
            out_specs=pl.BlockSpec(memory_space=pl.ANY),
            scratch_shapes=[pltpu.VMEM((2, tm, D), F32), pltpu.VMEM((tm, D), F32),
                            pltpu.SemaphoreType.DMA((2,)), pltpu.SemaphoreType.DMA((1,))]),
        out_shape=jax.ShapeDtypeStruct((2 * T, D), F32),
        compiler_params=_cparams("arbitrary"),
        name="moe_ffn",
    )(tile_e, n_tiles, n_valid, src, src, dst, h2, wg, wu, wd)


def _moe_out_kernel(x1_ref, y0_ref, y1_ref, wcol_ref, g2_ref, lg_ref, lb_ref, o_ref):
    w = wcol_ref[0]
    moe = w[:, 0:1] * y0_ref[...] + w[:, 1:2] * y1_ref[...]
    o_ref[0] = _ln(ALPHA * x1_ref[0] + g2_ref[0] * moe) * lg_ref[...] + lb_ref[...]


def _moe_out_call(x1, ycomb, wcol, mod, mod_row, ln_g, ln_b, tn):
    B, N, D = x1.shape
    nt = N // tn
    row = (lambda b: b) if mod_row is None else (lambda b: mod_row)
    vec = pl.BlockSpec((1, D), lambda b, i: (0, 0))
    return pl.pallas_call(
        _moe_out_kernel,
        grid=(B, nt),
        in_specs=[pl.BlockSpec((1, tn, D), lambda b, i: (b, i, 0)),
                  pl.BlockSpec((tn, D), lambda b, i: (b * nt + i, 0)),
                  pl.BlockSpec((tn, D), lambda b, i: (B * nt + b * nt + i, 0)),
                  pl.BlockSpec((1, tn, 128), lambda b, i: (b, i, 0)),
                  pl.BlockSpec((1, 1, D), lambda b, i: (row(b), 0, 5)),
                  vec, vec],
        out_specs=pl.BlockSpec((1, tn, D), lambda b, i: (b, i, 0)),
        out_shape=jax.ShapeDtypeStruct((B, N, D), F32),
        compiler_params=_cparams("parallel", "parallel"),
        name="moe_out",
    )(x1, ycomb, ycomb, wcol, mod, ln_g.reshape(1, D), ln_b.reshape(1, D))


def _moe_sparse(x1, h2, wcol, e12, mod, mod_row, wg, wu, wd, ln_g, ln_b, tn):
    B, N, D = x1.shape
    ycomb = _gffn_call(h2.reshape(B * N, D), _sort_tokens(e12, MOE_ROWS), wg, wu, wd, MOE_ROWS)
    return _moe_out_call(x1, ycomb, wcol, mod, mod_row, ln_g, ln_b, tn)


def _pair_swap(n):
    idx = np.arange(n)
    return idx + 1 - 2 * (idx % 2)


def _inproj_weights(w_in_l):
    av, ag, dq, dk, dv, fz, wq, wk, wv = jnp.split(w_in_l, np.cumsum([256, 256, 256, 256, 256, 256, 256, 128])[:8].tolist(), axis=1)
    dup = lambda w: jnp.concatenate([w[:, :WIN_DH], w[:, :WIN_DH], w[:, WIN_DH:], w[:, WIN_DH:]], axis=1)
    wkx, wvx = dup(wk), dup(wv)
    sw = _pair_swap(256)
    cols = [av, ag, dq, dk, dv, fz, wq, wkx, wvx, dq[:, sw], dk[:, sw], wq[:, sw], wkx[:, sw]]
    return jnp.concatenate(cols, axis=1).astype(BF16)


def _rope_tables(N):
    rows = N // GRID_W
    row_pos = jnp.repeat(jnp.arange(rows, dtype=F32), GRID_W)
    col_pos = jnp.tile(jnp.arange(GRID_W, dtype=F32), rows)
    out = []
    for dh in (DIFF_DH, WIN_DH):
        n_axis = dh // 4
        inv = ROPE_BASE ** (-jnp.arange(n_axis, dtype=F32) / n_axis)
        ang = jnp.concatenate([row_pos[:, None] * inv, col_pos[:, None] * inv], -1)
        cos = jnp.repeat(jnp.cos(ang), 2, axis=-1)
        sin = jnp.repeat(jnp.sin(ang), 2, axis=-1) * jnp.tile(jnp.array([-1.0, 1.0], F32), dh // 2)
        out += [jnp.tile(cos, (1, 256 // dh)), jnp.tile(sin, (1, 256 // dh))]
    return tuple(out)


def _pick_tile(n, pref):
    return pref if n % pref == 0 else n


def kernel(x, c, ctx, c_ctx, w_mod, b_mod, w_in, w_out, conv_w, conv_b, conv_norm_g, conv_norm_b, diff_lambda,
           diff_subln_g, win_sink, ln_mix_g, ln_mix_b, ln_ffn_g, ln_ffn_b, router_w, router_bias, exp_w_gate,
           exp_w_up, exp_w_down):
    B, N, D = x.shape
    C = ctx.shape[1]
    ctx_row = B
    pad_rows = (-(B + 1)) % 8
    cs = jnp.concatenate([c, c_ctx[None, :], jnp.zeros((pad_rows, D), F32)], axis=0)
    mod_all = _mod_call(cs, w_mod, b_mod)
    tables = _rope_tables(N)
    dft = {n: _dft_tables(n) for n in {N, C}}
    router_wt = router_w.T
    xc = ctx
    for l in range(DEPTH):
        need_ctx = l < DEPTH - 1
        mod = mod_all[l].reshape(-1, 1, 6 * D)
        lam_init = 0.8 - 0.6 * math.exp(-0.3 * l)
        w_ext = _inproj_weights(w_in[l])
        w_o = w_out[l].astype(BF16)
        wg, wu, wd = exp_w_gate[l].astype(BF16), exp_w_up[l].astype(BF16), exp_w_down[l].astype(BF16)

        u = _inproj_call(x, mod, None, w_ext, tables, _pick_tile(N, 512))
        uc = _inproj_call(xc, mod, ctx_row, w_ext[:, :U_WIDTH], None, _pick_tile(C, 256))

        def mixers(u_lat):
            src = u_lat if u_lat is not None else uc
            n = src.shape[1]
            return (_conv_call(src, conv_w[l], conv_b[l], conv_norm_g[l], conv_norm_b[l]),
                    _diff_call(u_lat, uc, diff_lambda[l], diff_subln_g[l], lam_init, _pick_tile(n, 512)),
                    _fourier_call(src, dft[n], _pick_tile(n, 512)),
                    _win_call(u_lat, uc, win_sink[l], _pick_tile(n, 256)))

        routed = _outproj_call(mixers(u), w_o, x, mod, None, ln_mix_g[l], ln_mix_b[l], router_wt, router_bias,
                               _pick_tile(N, 512))
        x = _moe_sparse(*routed, mod, None, wg, wu, wd, ln_ffn_g[l], ln_ffn_b[l], _pick_tile(N, 512))
        if need_ctx:
            routed = _outproj_call(mixers(None), w_o, xc, mod, ctx_row, ln_mix_g[l], ln_mix_b[l], router_wt,
                                   router_bias, _pick_tile(C, 256))
            xc = _moe_sparse(*routed, mod, ctx_row, wg, wu, wd, ln_ffn_g[l], ln_ffn_b[l], _pick_tile(C, 256))
    return x
```

```python
import functools
import math

import numpy as np
import jax
import jax.numpy as jnp
from jax import lax
from jax.experimental import pallas as pl
from jax.experimental.pallas import tpu as pltpu

F32 = jnp.float32
BF16 = jnp.bfloat16

D_MODEL = 1024
DEPTH = 2
GRID_W = 64
CONV_CH = 256
CONV_WIDTH = 31
DIFF_HEADS = 4
DIFF_DH = 32
FOURIER_GROUPS = 4
FOURIER_CH = 64
WIN_DH = 64
WIN_KV_HEADS = 2
WINDOW = 128
ROPE_BASE = 10000.0
N_EXPERTS = 16
N_EXPERT_GROUPS = 4
EXPERTS_PER_GROUP = 4
D_EXPERT = 512
ALPHA = (2 * DEPTH) ** 0.25
LN_EPS = 1e-5

U_AV, U_AG, U_DQ, U_DK, U_DV, U_FZ, U_WQ, U_WK, U_WV = (256 * i for i in range(9))
U_WIDTH = 9 * 256
ROPE_GROUPS = (U_DQ, U_DK, U_WQ, U_WK)
V7X_VMEM_LIMIT = 48 * 1024 * 1024


def _cparams(*sem):
    return pltpu.CompilerParams(dimension_semantics=sem, vmem_limit_bytes=V7X_VMEM_LIMIT)


def _ln(x):
    mu = jnp.mean(x, axis=-1, keepdims=True)
    xc = x - mu
    var = jnp.mean(xc * xc, axis=-1, keepdims=True)
    return xc * lax.rsqrt(var + LN_EPS)


def _sigmoid(x):
    return 1.0 / (1.0 + jnp.exp(-x))


def _dot(a, b):
    return jnp.dot(a, b, preferred_element_type=F32)


def _dot_nt(a, b):
    return lax.dot_general(a, b, (((1,), (1,)), ((), ())), preferred_element_type=F32)


def _mod_kernel(c_ref, w_ref, b_ref, o_ref):
    c = c_ref[...]
    s = c * _sigmoid(c)
    o_ref[0] = jnp.dot(s, w_ref[0], preferred_element_type=F32, precision=lax.Precision.HIGHEST) + b_ref[0]


def _mod_call(cs, w_mod, b_mod):
    R, D = cs.shape
    return pl.pallas_call(
        _mod_kernel,
        grid=(DEPTH, 6),
        in_specs=[pl.BlockSpec((R, D), lambda l, j: (0, 0)),
                  pl.BlockSpec((1, D, D), lambda l, j: (l, 0, j)),
                  pl.BlockSpec((1, 1, D), lambda l, j: (l, 0, j))],
        out_specs=pl.BlockSpec((1, R, D), lambda l, j: (l, 0, j)),
        out_shape=jax.ShapeDtypeStruct((DEPTH, R, 6 * D), F32),
        compiler_params=_cparams("arbitrary", "arbitrary"),
        name="mod",
    )(cs, w_mod, b_mod.reshape(DEPTH, 1, 6 * D))


def _inproj_kernel(*refs, rope):
    if rope:
        x_ref, sh_ref, sc_ref, w_ref, cd_ref, sd_ref, cw_ref, sw_ref, o_ref = refs
    else:
        x_ref, sh_ref, sc_ref, w_ref, o_ref = refs
    h = (_ln(x_ref[0]) * (1.0 + sc_ref[0]) + sh_ref[0]).astype(BF16)
    q_scale = {U_DQ: DIFF_DH ** -0.5 * math.log2(math.e), U_WQ: WIN_DH ** -0.5}
    n_sw = 0
    for a in range(0, U_WIDTH, 256):
        val = _dot(h, w_ref[:, a:a + 256])
        if a in q_scale:
            val = val * q_scale[a]
        if rope and a in ROPE_GROUPS:
            sw = _dot(h, w_ref[:, U_WIDTH + 256 * n_sw:U_WIDTH + 256 * (n_sw + 1)])
            if a in q_scale:
                sw = sw * q_scale[a]
            n_sw += 1
            cos, sin = (cd_ref, sd_ref) if a in (U_DQ, U_DK) else (cw_ref, sw_ref)
            val = val * cos[...] + sw * sin[...]
        o_ref[0, :, a:a + 256] = val.astype(BF16)


def _inproj_call(x, mod, mod_row, w, tables, tn):
    B, N, D = x.shape
    rope = tables is not None
    row = (lambda b: b) if mod_row is None else (lambda b: mod_row)
    in_specs = [pl.BlockSpec((1, tn, D), lambda b, i: (b, i, 0)),
                pl.BlockSpec((1, 1, D), lambda b, i: (row(b), 0, 0)),
                pl.BlockSpec((1, 1, D), lambda b, i: (row(b), 0, 1)),
                pl.BlockSpec(w.shape, lambda b, i: (0, 0))]
    args = [x, mod, mod, w]
    if rope:
        in_specs += [pl.BlockSpec((tn, 256), lambda b, i: (i, 0))] * 4
        args += list(tables)
    return pl.pallas_call(
        functools.partial(_inproj_kernel, rope=rope),
        grid=(B, N // tn),
        in_specs=in_specs,
        out_specs=pl.BlockSpec((1, tn, U_WIDTH), lambda b, i: (b, i, 0)),
        out_shape=jax.ShapeDtypeStruct((B, N, U_WIDTH), BF16),
        compiler_params=_cparams("parallel", "parallel"),
        name="inproj_rope" if rope else "inproj",
    )(*args)


CONV_ROWS = 128
CONV_PAD = 16


def _conv_kernel(u_ref, w_ref, b_ref, g_ref, nb_ref, o_ref, pad_ref, sh_ref):
    N = u_ref.shape[1]
    L = N + 2 * CONV_PAD
    val = u_ref[0, :, 0:CONV_CH].astype(F32)
    gate = u_ref[0, :, CONV_CH:2 * CONV_CH].astype(F32)
    pad_ref[0:CONV_PAD, :] = jnp.zeros((CONV_PAD, CONV_CH), F32)
    pad_ref[CONV_PAD + N:L + 8, :] = jnp.zeros((CONV_PAD + 8, CONV_CH), F32)
    pad_ref[CONV_PAD:CONV_PAD + N, :] = val * _sigmoid(gate)
    for s in range(8):
        sh_ref[s] = pad_ref[s:s + L, :]

    def tile(i, carry):
        r0 = pl.multiple_of(i * CONV_ROWS, CONV_ROWS)
        acc = jnp.zeros((CONV_ROWS, CONV_CH), F32)
        for k in range(CONV_WIDTH):
            off = CONV_PAD - CONV_WIDTH // 2 + k
            acc = acc + w_ref[k:k + 1, :] * sh_ref[off % 8, pl.ds(r0 + 8 * (off // 8), CONV_ROWS), :]
        y = _ln(acc + b_ref[...]) * g_ref[...] + nb_ref[...]
        o_ref[0, pl.ds(r0, CONV_ROWS), :] = (y * _sigmoid(y)).astype(BF16)
        return carry

    lax.fori_loop(0, N // CONV_ROWS, tile, 0)


def _conv_call(u, conv_w, conv_b, conv_ng, conv_nb):
    B, N, _ = u.shape
    vec = pl.BlockSpec((1, CONV_CH), lambda b: (0, 0))
    return pl.pallas_call(
        _conv_kernel,
        grid=(B,),
        in_specs=[pl.BlockSpec((1, N, 2 * CONV_CH), lambda b: (b, 0, 0)),
                  pl.BlockSpec((CONV_WIDTH, CONV_CH), lambda b: (0, 0)), vec, vec, vec],
        out_specs=pl.BlockSpec((1, N, CONV_CH), lambda b: (b, 0, 0)),
        out_shape=jax.ShapeDtypeStruct((B, N, CONV_CH), BF16),
        scratch_shapes=[pltpu.VMEM((N + 2 * CONV_PAD + 8, CONV_CH), F32),
                        pltpu.VMEM((8, N + 2 * CONV_PAD, CONV_CH), F32)],
        compiler_params=_cparams("parallel"),
        name="conv",
    )(u, conv_w, conv_b.reshape(1, -1), conv_ng.reshape(1, -1), conv_nb.reshape(1, -1))


def _diff_kernel(*refs, lam_init, has_lat):
    if has_lat:
        q_ref, k_ref, v_ref, kc_ref, vc_ref, dl_ref, g_ref, o_ref, kall_ref, vt_ref = refs
    else:
        q_ref, kc_ref, vc_ref, dl_ref, g_ref, o_ref, kall_ref, vt_ref = refs
    C = kc_ref.shape[1]
    DV = 2 * DIFF_DH

    @pl.when(pl.program_id(1) == 0)
    def _():
        kall_ref[0:C, :] = kc_ref[0]
        vt_ref[:, 0:C] = vc_ref[0].astype(F32).T.astype(BF16)
        if has_lat:
            N = k_ref.shape[1]
            kall_ref[C:C + N, :] = k_ref[0]
            vt_ref[:, C:C + N] = v_ref[0].astype(F32).T.astype(BF16)

    q = q_ref[0]
    kall = kall_ref[...]
    dl = dl_ref[...]
    lam = (jnp.exp(jnp.sum(dl[0:1] * dl[1:2], axis=-1, keepdims=True))
           - jnp.exp(jnp.sum(dl[2:3] * dl[3:4], axis=-1, keepdims=True)) + lam_init)
    lane = lax.broadcasted_iota(jnp.int32, (1, DIFF_HEADS * DV), 1)
    outs = []
    for h in range(DIFF_HEADS):
        parts = []
        for c in range(2):
            lo = (2 * h + c) * DIFF_DH
            km = jnp.where((lane >= lo) & (lane < lo + DIFF_DH), kall, jnp.zeros_like(kall))
            st = _dot_nt(km, q)
            e = jnp.exp2(st - jnp.max(st, axis=0, keepdims=True))
            parts.append((e, jnp.sum(e, axis=0, keepdims=True)))
        (e0, l0), (e1, l1) = parts
        at = (e0 - (lam * l0 / l1) * e1).astype(BF16)
        ot = _dot(vt_ref[DV * h:DV * (h + 1), :], at) * (1.0 / l0)
        ms = jnp.mean(ot * ot, axis=0, keepdims=True)
        outs.append(ot * lax.rsqrt(ms + LN_EPS))
    yt = jnp.concatenate(outs, axis=0) * g_ref[...] * (1.0 - lam_init)
    o_ref[0] = yt.T.astype(BF16)


def _diff_call(u, uc, diff_lambda, subln_g, lam_init, tq):
    has_lat = u is not None
    src = u if has_lat else uc
    B, N, _ = src.shape
    C = uc.shape[1]
    W = DIFF_HEADS * 2 * DIFF_DH
    n_keys = C + N if has_lat else C
    blk = lambda col: U_DQ // 256 + col
    in_specs = [pl.BlockSpec((1, tq, W), lambda b, i: (b, i, blk(0)))]
    args = [src]
    if has_lat:
        in_specs += [pl.BlockSpec((1, N, W), lambda b, i: (b, 0, blk(1))),
                     pl.BlockSpec((1, N, W), lambda b, i: (b, 0, blk(2)))]
        args += [u, u]
    in_specs += [pl.BlockSpec((1, C, W), lambda b, i: (b, 0, blk(1))),
                 pl.BlockSpec((1, C, W), lambda b, i: (b, 0, blk(2))),
                 pl.BlockSpec((4, DIFF_DH), lambda b, i: (0, 0)),
                 pl.BlockSpec((W, 1), lambda b, i: (0, 0))]
    args += [uc, uc, diff_lambda, jnp.tile(subln_g, DIFF_HEADS).reshape(W, 1)]
    return pl.pallas_call(
        functools.partial(_diff_kernel, lam_init=lam_init, has_lat=has_lat),
        grid=(B, N // tq),
        in_specs=in_specs,
        out_specs=pl.BlockSpec((1, tq, W), lambda b, i: (b, i, 0)),
        out_shape=jax.ShapeDtypeStruct((B, N, W), BF16),
        scratch_shapes=[pltpu.VMEM((n_keys, W), BF16), pltpu.VMEM((W, n_keys), BF16)],
        compiler_params=_cparams("parallel", "arbitrary"),
        name="diff_attn" if has_lat else "diff_attn_ctx",
    )(*args)


def _fourier_kernel(z_ref, cs_ref, cn_ref, sn_ref, o_ref, a_ref, b_ref, *, scale):
    W = FOURIER_GROUPS * FOURIER_CH

    @pl.when(pl.program_id(1) == 0)
    def _():
        t = _dot(z_ref[0], cs_ref[...])
        a_ref[...] = t[:, :W].astype(BF16)
        b_ref[...] = t[:, W:].astype(BF16)

    y = _dot(cn_ref[...], a_ref[...]) - _dot(sn_ref[...], b_ref[...])
    o_ref[0] = (y * scale).astype(BF16)


def _dft_tables(N):
    n = jnp.arange(N, dtype=jnp.int32)
    ang = ((n[:, None] * n[None, :]) % N).astype(F32) * (2.0 * math.pi / N)
    c = np.arange(FOURIER_CH, dtype=np.int64)
    angc = 2.0 * np.pi * ((c[:, None] * c[None, :]) % FOURIER_CH).astype(np.float64) / FOURIER_CH
    eye = np.eye(FOURIER_GROUPS)
    cs = np.concatenate([np.kron(eye, np.cos(angc)), np.kron(eye, np.sin(angc))], axis=1)
    return jnp.asarray(cs, F32).astype(BF16), jnp.cos(ang).astype(BF16), jnp.sin(ang).astype(BF16)


def _fourier_call(u, dft, tm):
    B, N, _ = u.shape
    W = FOURIER_GROUPS * FOURIER_CH
    cs, cn, sn = dft
    return pl.pallas_call(
        functools.partial(_fourier_kernel, scale=float((N * FOURIER_CH) ** -0.5)),
        grid=(B, N // tm),
        in_specs=[pl.BlockSpec((1, N, W), lambda b, i: (b, 0, U_FZ // 256)),
                  pl.BlockSpec((W, 2 * W), lambda b, i: (0, 0)),
                  pl.BlockSpec((tm, N), lambda b, i: (i, 0)),
                  pl.BlockSpec((tm, N), lambda b, i: (i, 0))],
        out_specs=pl.BlockSpec((1, tm, W), lambda b, i: (b, i, 0)),
        out_shape=jax.ShapeDtypeStruct((B, N, W), BF16),
        scratch_shapes=[pltpu.VMEM((N, W), BF16), pltpu.VMEM((N, W), BF16)],
        compiler_params=_cparams("parallel", "arbitrary"),
        name="fourier",
    )(u, cs, cn, sn)


def _win_kernel(*refs, tq, has_local):
    if has_local:
        sink_ref, q_ref, k_ref, v_ref, kc_ref, vc_ref, o_ref = refs
    else:
        sink_ref, q_ref, kc_ref, vc_ref, o_ref = refs
    g, i = pl.program_id(1), pl.program_id(2)
    q = q_ref[0]
    lane = lax.broadcasted_iota(jnp.int32, (1, 2 * WIN_DH), 1)
    lo = lane < WIN_DH
    zero = jnp.zeros_like(q)
    q2 = jnp.concatenate([jnp.where(lo, q, zero), jnp.where(lo, zero, q)], axis=0)
    row = lax.broadcasted_iota(jnp.int32, (2 * tq, 1), 0)
    sink = jnp.where(row < tq, sink_ref[2 * g], sink_ref[2 * g + 1])
    kc, vc = kc_ref[0], vc_ref[0]
    s_c = _dot_nt(q2, kc)
    m = jnp.maximum(jnp.max(s_c, axis=-1, keepdims=True), sink)
    if has_local:
        N = k_ref.shape[1]
        span = tq + 2 * WINDOW
        start = pl.multiple_of(jnp.clip(i * tq - WINDOW, 0, N - span), WINDOW)
        kw = k_ref[0, pl.ds(start, span), :]
        vw = v_ref[0, pl.ds(start, span), :]
        s_l = _dot_nt(q2, kw)
        qpos = i * tq + jnp.where(row < tq, row, row - tq)
        kpos = start + lax.broadcasted_iota(jnp.int32, (1, span), 1)
        s_l = jnp.where(jnp.abs(kpos - qpos) <= WINDOW, s_l, -jnp.inf)
        m = jnp.maximum(m, jnp.max(s_l, axis=-1, keepdims=True))
    e_c = jnp.exp(s_c - m)
    den = jnp.sum(e_c, axis=-1, keepdims=True) + jnp.exp(sink - m)
    if has_local:
        e_l = jnp.exp(s_l - m)
        den = den + jnp.sum(e_l, axis=-1, keepdims=True)
    inv = 1.0 / den
    o2 = _dot((e_c * inv).astype(BF16), vc)
    if has_local:
        o2 = o2 + _dot((e_l * inv).astype(BF16), vw)
    o_ref[0] = jnp.where(lo, o2[:tq], o2[tq:]).astype(BF16)


def _win_call(u, uc, sink, tq):
    has_local = u is not None
    src = u if has_local else uc
    B, N, _ = src.shape
    C = uc.shape[1]
    W = 2 * WIN_DH
    G = WIN_KV_HEADS
    in_specs = [pl.BlockSpec(memory_space=pltpu.SMEM),
                pl.BlockSpec((1, tq, W), lambda b, g, i: (b, i, U_WQ // W + g))]
    args = [sink, src]
    if has_local:
        in_specs += [pl.BlockSpec((1, N, W), lambda b, g, i: (b, 0, U_WK // W + g)),
                     pl.BlockSpec((1, N, W), lambda b, g, i: (b, 0, U_WV // W + g))]
        args += [u, u]
    in_specs += [pl.BlockSpec((1, C, W), lambda b, g, i: (b, 0, U_WK // W + g)),
                 pl.BlockSpec((1, C, W), lambda b, g, i: (b, 0, U_WV // W + g))]
    args += [uc, uc]
    return pl.pallas_call(
        functools.partial(_win_kernel, tq=tq, has_local=has_local),
        grid=(B, G, N // tq),
        in_specs=in_specs,
        out_specs=pl.BlockSpec((1, tq, W), lambda b, g, i: (b, i, g)),
        out_shape=jax.ShapeDtypeStruct((B, N, G * W), BF16),
        compiler_params=_cparams("parallel", "parallel", "parallel"),
        name="win_attn" if has_local else "win_attn_ctx",
    )(*args)


def _route(sel, score):
    EG = EXPERTS_PER_GROUP
    rows = [sel[e:e + 1] for e in range(N_EXPERTS)]
    srow = [score[e:e + 1] for e in range(N_EXPERTS)]
    best, best_score = None, None
    for g in range(N_EXPERT_GROUPS):
        r = rows[EG * g:EG * (g + 1)]
        top2 = None
        for a in range(EG):
            for b in range(a + 1, EG):
                pair = r[a] + r[b]
                top2 = pair if top2 is None else jnp.maximum(top2, pair)
        if g == 0:
            best, best_score = jnp.zeros(top2.shape, jnp.int32), top2
        else:
            upd = top2 > best_score
            best = jnp.where(upd, g, best)
            best_score = jnp.where(upd, top2, best_score)

    def pick(vals, j):
        out = vals[j]
        for g in range(1, N_EXPERT_GROUPS):
            out = jnp.where(best == g, vals[EG * g + j], out)
        return out

    neg = jnp.full(best_score.shape, -jnp.inf, F32)
    m1, m2 = neg, neg
    i1 = i2 = jnp.zeros(best.shape, jnp.int32)
    w1 = w2 = jnp.zeros(best_score.shape, F32)
    for j in range(EG):
        v, sc = pick(rows, j), pick(srow, j)
        gt1 = v > m1
        gt2 = v > m2
        m2 = jnp.where(gt1, m1, jnp.where(gt2, v, m2))
        i2 = jnp.where(gt1, i1, jnp.where(gt2, j, i2))
        w2 = jnp.where(gt1, w1, jnp.where(gt2, sc, w2))
        m1 = jnp.where(gt1, v, m1)
        i1 = jnp.where(gt1, j, i1)
        w1 = jnp.where(gt1, sc, w1)
    tot = w1 + w2
    return best * EG + i1, best * EG + i2, w1 / tot, w2 / tot


def _outproj_kernel(ya_ref, yb_ref, yc_ref, yd_ref, w_ref, x_ref, g1_ref, sh2_ref, sc2_ref, lg_ref, lb_ref,
                    rw_ref, rb_ref, x1_ref, h2_ref, wcol_ref, e_ref, cnt_ref, seen_ref):
    tn = x_ref.shape[1]
    i = pl.program_id(1)

    @pl.when(i == 0)
    def _():
        seen_ref[...] = jnp.zeros_like(seen_ref)

    y = _dot(ya_ref[0], w_ref[0:256, :])
    y = y + _dot(yb_ref[0], w_ref[256:512, :])
    y = y + _dot(yc_ref[0], w_ref[512:768, :])
    y = y + _dot(yd_ref[0], w_ref[768:1024, :])
    x1 = _ln(ALPHA * x_ref[0] + g1_ref[0] * y) * lg_ref[...] + lb_ref[...]
    x1_ref[0] = x1
    h2 = _ln(x1) * (1.0 + sc2_ref[0]) + sh2_ref[0]
    logits = lax.dot_general(rw_ref[...], h2, (((1,), (1,)), ((), ())), preferred_element_type=F32,
                             precision=lax.Precision.HIGHEST)
    h2_ref[0] = h2
    score = _sigmoid(logits)
    e1, e2, w1, w2 = _route(score + rb_ref[...], score)
    wcol_ref[0] = jnp.concatenate([w1, w2, jnp.zeros((126, tn), F32)], axis=0).T
    erow = lax.broadcasted_iota(jnp.int32, (N_EXPERTS, tn), 0)
    hit1, hit2 = erow == e1, erow == e2
    chosen = jnp.where(hit1, 1.0, 0.0) + jnp.where(hit2, 1.0, 0.0)
    earlier = jnp.where(lax.broadcasted_iota(jnp.int32, (tn, tn), 0) < lax.broadcasted_iota(jnp.int32, (tn, tn), 1),
                        1.0, 0.0).astype(BF16)
    before = _dot(chosen.astype(BF16), earlier) + seen_ref[...]
    r1 = jnp.sum(jnp.where(hit1, before, 0.0), axis=0, keepdims=True)
    r2 = jnp.sum(jnp.where(hit2, before, 0.0), axis=0, keepdims=True)
    e_ref[0] = jnp.concatenate([e1, e2, r1.astype(jnp.int32), r2.astype(jnp.int32), jnp.zeros((4, tn), jnp.int32)],
                               axis=0)
    seen_ref[...] += jnp.sum(chosen, axis=1, keepdims=True)

    @pl.when(i == pl.num_programs(1) - 1)
    def _():
        cnt_ref[0] = jnp.broadcast_to(seen_ref[...], (N_EXPERTS, 128))


def _outproj_call(ys, w_out, x, mod, mod_row, ln_g, ln_b, router_wt, router_bias, tn):
    B, N, D = x.shape
    row = (lambda b: b) if mod_row is None else (lambda b: mod_row)
    yspec = pl.BlockSpec((1, tn, 256), lambda b, i: (b, i, 0))
    mspec = lambda j: pl.BlockSpec((1, 1, D), lambda b, i: (row(b), 0, j))
    vec = pl.BlockSpec((1, D), lambda b, i: (0, 0))
    return pl.pallas_call(
        _outproj_kernel,
        grid=(B, N // tn),
        in_specs=[yspec, yspec, yspec, yspec,
                  pl.BlockSpec((D, D), lambda b, i: (0, 0)),
                  pl.BlockSpec((1, tn, D), lambda b, i: (b, i, 0)),
                  mspec(2), mspec(3), mspec(4), vec, vec,
                  pl.BlockSpec((N_EXPERTS, D), lambda b, i: (0, 0)),
                  pl.BlockSpec((N_EXPERTS, 1), lambda b, i: (0, 0))],
        out_specs=[pl.BlockSpec((1, tn, D), lambda b, i: (b, i, 0)),
                   pl.BlockSpec((1, tn, D), lambda b, i: (b, i, 0)),
                   pl.BlockSpec((1, tn, 128), lambda b, i: (b, i, 0)),
                   pl.BlockSpec((1, 8, tn), lambda b, i: (b, 0, i)),
                   pl.BlockSpec((1, N_EXPERTS, 128), lambda b, i: (b, 0, 0))],
        out_shape=[jax.ShapeDtypeStruct((B, N, D), F32), jax.ShapeDtypeStruct((B, N, D), F32),
                   jax.ShapeDtypeStruct((B, N, 128), F32), jax.ShapeDtypeStruct((B, 8, N), jnp.int32),
                   jax.ShapeDtypeStruct((B, N_EXPERTS, 128), F32)],
        scratch_shapes=[pltpu.VMEM((N_EXPERTS, 1), F32)],
        compiler_params=_cparams("parallel", "arbitrary"),
        name="outproj_router",
    )(*ys, w_out, x, mod, mod, mod, ln_g.reshape(1, D), ln_b.reshape(1, D), router_wt, router_bias.reshape(-1, 1))


MOE_ROWS = 512


def _slot_table(er, cnt, tm, tn):
    B, _, N = er.shape
    E = N_EXPERTS
    c = cnt[:, :, 0].astype(jnp.int32)
    counts = jnp.sum(c, axis=0)
    ptiles = (counts + tm - 1) // tm
    pend = jnp.cumsum(ptiles) * tm
    base = (pend - ptiles * tm)[None, :] + jnp.cumsum(c, axis=0) - c
    eids = jnp.arange(E, dtype=jnp.int32)[None, :, None]
    slots = [jnp.sum(jnp.where(er[:, k, None, :] == eids, base[:, :, None], 0), axis=1) + er[:, 2 + k, :]
             for k in range(2)]
    slots = jnp.stack(slots, axis=1).reshape(B, 2, N // tn, tn).transpose(0, 2, 1, 3).reshape(-1, 2, tn)
    tile0 = jnp.arange((2 * B * N + E * tm) // tm, dtype=jnp.int32) * tm
    tile_e = jnp.minimum(jnp.sum((pend[None, :] <= tile0[:, None]).astype(jnp.int32), axis=1), E - 1)
    return slots.astype(jnp.int32), tile_e, jnp.sum(ptiles).astype(jnp.int32).reshape(1)


def _dispatch_kernel(slot_ref, h_ref, xs_init, xs_hbm, sem):
    del xs_init
    tn = h_ref.shape[1]
    for r in range(tn):
        for k in range(2):
            pltpu.make_async_copy(h_ref.at[0, pl.ds(r, 1)], xs_hbm.at[pl.ds(slot_ref[0, k, r], 1)], sem.at[0]).start()
    for k in range(2):
        pltpu.make_async_copy(h_ref.at[0], xs_hbm.at[pl.ds(0, tn)], sem.at[0]).wait()


def _dispatch_call(h2, slots, n_rows, tn):
    B, N, D = h2.shape
    nt = N // tn
    return pl.pallas_call(
        _dispatch_kernel,
        grid=(B, nt),
        in_specs=[pl.BlockSpec((1, 2, tn), lambda b, i: (b * nt + i, 0, 0), memory_space=pltpu.SMEM),
                  pl.BlockSpec((1, tn, D), lambda b, i: (b, i, 0)),
                  pl.BlockSpec(memory_space=pl.ANY)],
        out_specs=pl.BlockSpec(memory_space=pl.ANY),
        out_shape=jax.ShapeDtypeStruct((n_rows, D), F32),
        scratch_shapes=[pltpu.SemaphoreType.DMA((1,))],
        input_output_aliases={2: 0},
        compiler_params=_cparams("arbitrary", "arbitrary"),
        name="moe_dispatch",
    )(slots, h2, jnp.zeros((n_rows, D), F32))


def _gffn_kernel(te_ref, nt_ref, x_ref, wg_ref, wu_ref, wd_ref, o_ref):
    del te_ref
    i = pl.program_id(0)

    @pl.when(i < nt_ref[0])
    def _():
        x = x_ref[...].astype(BF16)
        gate = _dot(x, wg_ref[0])
        up = _dot(x, wu_ref[0])
        act = (gate * _sigmoid(gate) * up).astype(BF16)
        o_ref[...] = _dot(act, wd_ref[0])

    @pl.when(i >= nt_ref[0])
    def _():
        o_ref[...] = jnp.zeros_like(o_ref)


def _gffn_call(xs, tile_e, n_tiles, wg, wu, wd, tm):
    S, D = xs.shape
    F = wg.shape[-1]
    wspec = lambda shape: pl.BlockSpec(shape, lambda i, te, nt: (te[i], 0, 0))
    return pl.pallas_call(
        _gffn_kernel,
        grid_spec=pltpu.PrefetchScalarGridSpec(
            num_scalar_prefetch=2,
            grid=(S // tm,),
            in_specs=[pl.BlockSpec((tm, D), lambda i, te, nt: (i, 0)),
                      wspec((1, D, F)), wspec((1, D, F)), wspec((1, F, D))],
            out_specs=pl.BlockSpec((tm, D), lambda i, te, nt: (i, 0))),
        out_shape=jax.ShapeDtypeStruct((S, D), F32),
        compiler_params=_cparams("arbitrary"),
        name="moe_ffn",
    )(tile_e, n_tiles, xs, wg, wu, wd)


def _moe_out_kernel(slot_ref, slotn_ref, x1_ref, wcol_ref, g2_ref, lg_ref, lb_ref, ys_hbm, o_ref, ybuf, sem):
    t = pl.program_id(0)
    cur = t % 2
    tn = x1_ref.shape[1]

    def issue(s_ref, buf):
        for r in range(tn):
            for k in range(2):
                pltpu.make_async_copy(ys_hbm.at[pl.ds(s_ref[0, k, r], 1)], ybuf.at[buf, k, pl.ds(r, 1)],
                                      sem.at[buf]).start()

    @pl.when(t == 0)
    def _():
        issue(slot_ref, 0)

    @pl.when(t + 1 < pl.num_programs(0))
    def _():
        issue(slotn_ref, 1 - cur)

    for k in range(2):
        pltpu.make_async_copy(ys_hbm.at[pl.ds(0, tn)], ybuf.at[cur, k], sem.at[cur]).wait()
    w = wcol_ref[0]
    moe = w[:, 0:1] * ybuf[cur, 0] + w[:, 1:2] * ybuf[cur, 1]
    o_ref[0] = _ln(ALPHA * x1_ref[0] + g2_ref[0] * moe) * lg_ref[...] + lb_ref[...]


def _moe_out_call(x1, ys, slots, wcol, mod, mod_row, ln_g, ln_b, tn):
    B, N, D = x1.shape
    nt = N // tn
    last = B * nt - 1
    row = (lambda t: t // nt) if mod_row is None else (lambda t: mod_row)
    sspec = lambda f: pl.BlockSpec((1, 2, tn), lambda t: (f(t), 0, 0), memory_space=pltpu.SMEM)
    vec = pl.BlockSpec((1, D), lambda t: (0, 0))
    return pl.pallas_call(
        _moe_out_kernel,
        grid=(B * nt,),
        in_specs=[sspec(lambda t: t), sspec(lambda t: jnp.minimum(t + 1, last)),
                  pl.BlockSpec((1, tn, D), lambda t: (t // nt, t % nt, 0)),
                  pl.BlockSpec((1, tn, 128), lambda t: (t // nt, t % nt, 0)),
                  pl.BlockSpec((1, 1, D), lambda t: (row(t), 0, 5)),
                  vec, vec,
                  pl.BlockSpec(memory_space=pl.ANY)],
        out_specs=pl.BlockSpec((1, tn, D), lambda t: (t // nt, t % nt, 0)),
        out_shape=jax.ShapeDtypeStruct((B, N, D), F32),
        scratch_shapes=[pltpu.VMEM((2, 2, tn, D), F32), pltpu.SemaphoreType.DMA((2,))],
        compiler_params=_cparams("arbitrary"),
        name="moe_out",
    )(slots, slots, x1, wcol, mod, ln_g.reshape(1, D), ln_b.reshape(1, D), ys)


def _moe_sparse(x1, h2, wcol, er, cnt, mod, mod_row, wg, wu, wd, ln_g, ln_b, tn):
    B, N, D = x1.shape
    slots, tile_e, n_tiles = _slot_table(er, cnt, MOE_ROWS, tn)
    xs = _dispatch_call(h2, slots, 2 * B * N + N_EXPERTS * MOE_ROWS, tn)
    ys = _gffn_call(xs, tile_e, n_tiles, wg, wu, wd, MOE_ROWS)
    return _moe_out_call(x1, ys, slots, wcol, mod, mod_row, ln_g, ln_b, tn)


def _pair_swap(n):
    idx = np.arange(n)
    return idx + 1 - 2 * (idx % 2)


def _inproj_weights(w_in_l):
    av, ag, dq, dk, dv, fz, wq, wk, wv = jnp.split(w_in_l, np.cumsum([256, 256, 256, 256, 256, 256, 256, 128])[:8].tolist(), axis=1)
    dup = lambda w: jnp.concatenate([w[:, :WIN_DH], w[:, :WIN_DH], w[:, WIN_DH:], w[:, WIN_DH:]], axis=1)
    wkx, wvx = dup(wk), dup(wv)
    sw = _pair_swap(256)
    cols = [av, ag, dq, dk, dv, fz, wq, wkx, wvx, dq[:, sw], dk[:, sw], wq[:, sw], wkx[:, sw]]
    return jnp.concatenate(cols, axis=1).astype(BF16)


def _rope_tables(N):
    rows = N // GRID_W
    row_pos = jnp.repeat(jnp.arange(rows, dtype=F32), GRID_W)
    col_pos = jnp.tile(jnp.arange(GRID_W, dtype=F32), rows)
    out = []
    for dh in (DIFF_DH, WIN_DH):
        n_axis = dh // 4
        inv = ROPE_BASE ** (-jnp.arange(n_axis, dtype=F32) / n_axis)
        ang = jnp.concatenate([row_pos[:, None] * inv, col_pos[:, None] * inv], -1)
        cos = jnp.repeat(jnp.cos(ang), 2, axis=-1)
        sin = jnp.repeat(jnp.sin(ang), 2, axis=-1) * jnp.tile(jnp.array([-1.0, 1.0], F32), dh // 2)
        out += [jnp.tile(cos, (1, 256 // dh)), jnp.tile(sin, (1, 256 // dh))]
    return tuple(out)


def _pick_tile(n, pref):
    return pref if n % pref == 0 else n


def kernel(x, c, ctx, c_ctx, w_mod, b_mod, w_in, w_out, conv_w, conv_b, conv_norm_g, conv_norm_b, diff_lambda,
           diff_subln_g, win_sink, ln_mix_g, ln_mix_b, ln_ffn_g, ln_ffn_b, router_w, router_bias, exp_w_gate,
           exp_w_up, exp_w_down):
    B, N, D = x.shape
    C = ctx.shape[1]
    ctx_row = B
    pad_rows = (-(B + 1)) % 8
    cs = jnp.concatenate([c, c_ctx[None, :], jnp.zeros((pad_rows, D), F32)], axis=0)
    mod_all = _mod_call(cs, w_mod, b_mod)
    tables = _rope_tables(N)
    dft = {n: _dft_tables(n) for n in {N, C}}
    router_wt = router_w.T
    xc = ctx
    for l in range(DEPTH):
        need_ctx = l < DEPTH - 1
        mod = mod_all[l].reshape(-1, 1, 6 * D)
        lam_init = 0.8 - 0.6 * math.exp(-0.3 * l)
        w_ext = _inproj_weights(w_in[l])
        w_o = w_out[l].astype(BF16)
        wg, wu, wd = exp_w_gate[l].astype(BF16), exp_w_up[l].astype(BF16), exp_w_down[l].astype(BF16)

        u = _inproj_call(x, mod, None, w_ext, tables, _pick_tile(N, 512))
        uc = _inproj_call(xc, mod, ctx_row, w_ext[:, :U_WIDTH], None, _pick_tile(C, 256))

        def mixers(u_lat):
            src = u_lat if u_lat is not None else uc
            n = src.shape[1]
            return (_conv_call(src, conv_w[l], conv_b[l], conv_norm_g[l], conv_norm_b[l]),
                    _diff_call(u_lat, uc, diff_lambda[l], diff_subln_g[l], lam_init, _pick_tile(n, 512)),
                    _fourier_call(src, dft[n], _pick_tile(n, 512)),
                    _win_call(u_lat, uc, win_sink[l], _pick_tile(n, 256)))

        routed = _outproj_call(mixers(u), w_o, x, mod, None, ln_mix_g[l], ln_mix_b[l], router_wt, router_bias,
                               _pick_tile(N, 512))
        x = _moe_sparse(*routed, mod, None, wg, wu, wd, ln_ffn_g[l], ln_ffn_b[l], _pick_tile(N, 512))
        if need_ctx:
            routed = _outproj_call(mixers(None), w_o, xc, mod, ctx_row, ln_mix_g[l], ln_mix_b[l], router_wt,
                                   router_bias, _pick_tile(C, 256))
            xc = _moe_sparse(*routed, mod, ctx_row, wg, wu, wd, ln_ffn_g[l], ln_ffn_b[l], _pick_tile(C, 256))
    return x
```

```python
import functools
import math

import numpy as np
import jax
import jax.numpy as jnp
from jax import lax
from jax.experimental import pallas as pl
from jax.experimental.pallas import tpu as pltpu

F32 = jnp.float32
BF16 = jnp.bfloat16

D_MODEL = 1024
DEPTH = 2
GRID_W = 64
CONV_CH = 256
CONV_WIDTH = 31
DIFF_HEADS = 4
DIFF_DH = 32
FOURIER_GROUPS = 4
FOURIER_CH = 64
WIN_DH = 64
WIN_KV_HEADS = 2
WINDOW = 128
ROPE_BASE = 10000.0
N_EXPERTS = 16
N_EXPERT_GROUPS = 4
EXPERTS_PER_GROUP = 4
D_EXPERT = 512
ALPHA = (2 * DEPTH) ** 0.25
LN_EPS = 1e-5

U_AV, U_AG, U_DQ, U_DK, U_DV, U_FZ, U_WQ, U_WK, U_WV = (256 * i for i in range(9))
U_WIDTH = 9 * 256
ROPE_GROUPS = (U_DQ, U_DK, U_WQ, U_WK)
V7X_VMEM_LIMIT = 48 * 1024 * 1024


def _cparams(*sem):
    return pltpu.CompilerParams(dimension_semantics=sem, vmem_limit_bytes=V7X_VMEM_LIMIT)


def _ln(x):
    mu = jnp.mean(x, axis=-1, keepdims=True)
    xc = x - mu
    var = jnp.mean(xc * xc, axis=-1, keepdims=True)
    return xc * lax.rsqrt(var + LN_EPS)


def _sigmoid(x):
    return 1.0 / (1.0 + jnp.exp(-x))


def _dot(a, b):
    return jnp.dot(a, b, preferred_element_type=F32)


def _dot_nt(a, b):
    return lax.dot_general(a, b, (((1,), (1,)), ((), ())), preferred_element_type=F32)


def _mod_kernel(c_ref, w_ref, b_ref, o_ref):
    c = c_ref[...]
    s = c * _sigmoid(c)
    o_ref[0] = jnp.dot(s, w_ref[0], preferred_element_type=F32, precision=lax.Precision.HIGHEST) + b_ref[0]


def _mod_call(cs, w_mod, b_mod):
    R, D = cs.shape
    return pl.pallas_call(
        _mod_kernel,
        grid=(DEPTH, 6),
        in_specs=[pl.BlockSpec((R, D), lambda l, j: (0, 0)),
                  pl.BlockSpec((1, D, D), lambda l, j: (l, 0, j)),
                  pl.BlockSpec((1, 1, D), lambda l, j: (l, 0, j))],
        out_specs=pl.BlockSpec((1, R, D), lambda l, j: (l, 0, j)),
        out_shape=jax.ShapeDtypeStruct((DEPTH, R, 6 * D), F32),
        compiler_params=_cparams("arbitrary", "arbitrary"),
        name="mod",
    )(cs, w_mod, b_mod.reshape(DEPTH, 1, 6 * D))


def _inproj_kernel(*refs, rope):
    if rope:
        x_ref, sh_ref, sc_ref, w_ref, cd_ref, sd_ref, cw_ref, sw_ref, o_ref = refs
    else:
        x_ref, sh_ref, sc_ref, w_ref, o_ref = refs
    h = (_ln(x_ref[0]) * (1.0 + sc_ref[0]) + sh_ref[0]).astype(BF16)
    q_scale = {U_DQ: DIFF_DH ** -0.5 * math.log2(math.e), U_WQ: WIN_DH ** -0.5}
    even = lax.broadcasted_iota(jnp.int32, (1, 128), 1) % 2 == 0
    for a in range(0, U_WIDTH, 256):
        val = _dot(h, w_ref[:, a:a + 256])
        if a in q_scale:
            val = val * q_scale[a]
        if rope and a in ROPE_GROUPS:
            cos, sin = (cd_ref, sd_ref) if a in (U_DQ, U_DK) else (cw_ref, sw_ref)
            for half in range(2):
                v = val[:, 128 * half:128 * (half + 1)]
                partner = jnp.where(even, pltpu.roll(v, 127, axis=1), pltpu.roll(v, 1, axis=1))
                o_ref[0, :, a + 128 * half:a + 128 * (half + 1)] = (
                    v * cos[:, 128 * half:128 * (half + 1)] + partner * sin[:, 128 * half:128 * (half + 1)]).astype(BF16)
        else:
            o_ref[0, :, a:a + 256] = val.astype(BF16)


def _inproj_call(x, mod, mod_row, w, tables, tn):
    B, N, D = x.shape
    rope = tables is not None
    row = (lambda b: b) if mod_row is None else (lambda b: mod_row)
    in_specs = [pl.BlockSpec((1, tn, D), lambda b, i: (b, i, 0)),
                pl.BlockSpec((1, 1, D), lambda b, i: (row(b), 0, 0)),
                pl.BlockSpec((1, 1, D), lambda b, i: (row(b), 0, 1)),
                pl.BlockSpec(w.shape, lambda b, i: (0, 0))]
    args = [x, mod, mod, w]
    if rope:
        in_specs += [pl.BlockSpec((tn, 256), lambda b, i: (i, 0))] * 4
        args += list(tables)
    return pl.pallas_call(
        functools.partial(_inproj_kernel, rope=rope),
        grid=(B, N // tn),
        in_specs=in_specs,
        out_specs=pl.BlockSpec((1, tn, U_WIDTH), lambda b, i: (b, i, 0)),
        out_shape=jax.ShapeDtypeStruct((B, N, U_WIDTH), BF16),
        compiler_params=_cparams("parallel", "parallel"),
        name="inproj_rope" if rope else "inproj",
    )(*args)


CONV_ROWS = 128
CONV_PAD = 16


def _conv_kernel(u_ref, w_ref, b_ref, g_ref, nb_ref, o_ref, pad_ref, sh_ref):
    N = u_ref.shape[1]
    L = N + 2 * CONV_PAD
    val = u_ref[0, :, 0:CONV_CH].astype(F32)
    gate = u_ref[0, :, CONV_CH:2 * CONV_CH].astype(F32)
    pad_ref[0:CONV_PAD, :] = jnp.zeros((CONV_PAD, CONV_CH), F32)
    pad_ref[CONV_PAD + N:L + 8, :] = jnp.zeros((CONV_PAD + 8, CONV_CH), F32)
    pad_ref[CONV_PAD:CONV_PAD + N, :] = val * _sigmoid(gate)
    for s in range(8):
        sh_ref[s] = pad_ref[s:s + L, :]

    def tile(i, carry):
        r0 = pl.multiple_of(i * CONV_ROWS, CONV_ROWS)
        acc = jnp.zeros((CONV_ROWS, CONV_CH), F32)
        for k in range(CONV_WIDTH):
            off = CONV_PAD - CONV_WIDTH // 2 + k
            acc = acc + w_ref[k:k + 1, :] * sh_ref[off % 8, pl.ds(r0 + 8 * (off // 8), CONV_ROWS), :]
        y = _ln(acc + b_ref[...]) * g_ref[...] + nb_ref[...]
        o_ref[0, pl.ds(r0, CONV_ROWS), :] = (y * _sigmoid(y)).astype(BF16)
        return carry

    lax.fori_loop(0, N // CONV_ROWS, tile, 0)


def _conv_call(u, conv_w, conv_b, conv_ng, conv_nb):
    B, N, _ = u.shape
    vec = pl.BlockSpec((1, CONV_CH), lambda b: (0, 0))
    return pl.pallas_call(
        _conv_kernel,
        grid=(B,),
        in_specs=[pl.BlockSpec((1, N, 2 * CONV_CH), lambda b: (b, 0, 0)),
                  pl.BlockSpec((CONV_WIDTH, CONV_CH), lambda b: (0, 0)), vec, vec, vec],
        out_specs=pl.BlockSpec((1, N, CONV_CH), lambda b: (b, 0, 0)),
        out_shape=jax.ShapeDtypeStruct((B, N, CONV_CH), BF16),
        scratch_shapes=[pltpu.VMEM((N + 2 * CONV_PAD + 8, CONV_CH), F32),
                        pltpu.VMEM((8, N + 2 * CONV_PAD, CONV_CH), F32)],
        compiler_params=_cparams("parallel"),
        name="conv",
    )(u, conv_w, conv_b.reshape(1, -1), conv_ng.reshape(1, -1), conv_nb.reshape(1, -1))


def _diff_kernel(*refs, lam_init, has_lat):
    if has_lat:
        q_ref, k_ref, v_ref, kc_ref, vc_ref, dl_ref, g_ref, o_ref, kall_ref, vt_ref = refs
    else:
        q_ref, kc_ref, vc_ref, dl_ref, g_ref, o_ref, kall_ref, vt_ref = refs
    C = kc_ref.shape[1]
    DV = 2 * DIFF_DH

    @pl.when(pl.program_id(1) == 0)
    def _():
        kall_ref[0:C, :] = kc_ref[0]
        vt_ref[:, 0:C] = vc_ref[0].astype(F32).T.astype(BF16)
        if has_lat:
            N = k_ref.shape[1]
            kall_ref[C:C + N, :] = k_ref[0]
            vt_ref[:, C:C + N] = v_ref[0].astype(F32).T.astype(BF16)

    q = q_ref[0]
    kall = kall_ref[...]
    dl = dl_ref[...]
    lam = (jnp.exp(jnp.sum(dl[0:1] * dl[1:2], axis=-1, keepdims=True))
           - jnp.exp(jnp.sum(dl[2:3] * dl[3:4], axis=-1, keepdims=True)) + lam_init)
    lane = lax.broadcasted_iota(jnp.int32, (1, DIFF_HEADS * DV), 1)
    outs = []
    for h in range(DIFF_HEADS):
        parts = []
        for c in range(2):
            lo = (2 * h + c) * DIFF_DH
            km = jnp.where((lane >= lo) & (lane < lo + DIFF_DH), kall, jnp.zeros_like(kall))
            st = _dot_nt(km, q)
            e = jnp.exp2(st - jnp.max(st, axis=0, keepdims=True))
            parts.append((e, jnp.sum(e, axis=0, keepdims=True)))
        (e0, l0), (e1, l1) = parts
        at = (e0 - (lam * l0 / l1) * e1).astype(BF16)
        ot = _dot(vt_ref[DV * h:DV * (h + 1), :], at) * (1.0 / l0)
        ms = jnp.mean(ot * ot, axis=0, keepdims=True)
        outs.append(ot * lax.rsqrt(ms + LN_EPS))
    yt = jnp.concatenate(outs, axis=0) * g_ref[...] * (1.0 - lam_init)
    o_ref[0] = yt.T.astype(BF16)


def _diff_call(u, uc, diff_lambda, subln_g, lam_init, tq):
    has_lat = u is not None
    src = u if has_lat else uc
    B, N, _ = src.shape
    C = uc.shape[1]
    W = DIFF_HEADS * 2 * DIFF_DH
    n_keys = C + N if has_lat else C
    blk = lambda col: U_DQ // 256 + col
    in_specs = [pl.BlockSpec((1, tq, W), lambda b, i: (b, i, blk(0)))]
    args = [src]
    if has_lat:
        in_specs += [pl.BlockSpec((1, N, W), lambda b, i: (b, 0, blk(1))),
                     pl.BlockSpec((1, N, W), lambda b, i: (b, 0, blk(2)))]
        args += [u, u]
    in_specs += [pl.BlockSpec((1, C, W), lambda b, i: (b, 0, blk(1))),
                 pl.BlockSpec((1, C, W), lambda b, i: (b, 0, blk(2))),
                 pl.BlockSpec((4, DIFF_DH), lambda b, i: (0, 0)),
                 pl.BlockSpec((W, 1), lambda b, i: (0, 0))]
    args += [uc, uc, diff_lambda, jnp.tile(subln_g, DIFF_HEADS).reshape(W, 1)]
    return pl.pallas_call(
        functools.partial(_diff_kernel, lam_init=lam_init, has_lat=has_lat),
        grid=(B, N // tq),
        in_specs=in_specs,
        out_specs=pl.BlockSpec((1, tq, W), lambda b, i: (b, i, 0)),
        out_shape=jax.ShapeDtypeStruct((B, N, W), BF16),
        scratch_shapes=[pltpu.VMEM((n_keys, W), BF16), pltpu.VMEM((W, n_keys), BF16)],
        compiler_params=_cparams("parallel", "arbitrary"),
        name="diff_attn" if has_lat else "diff_attn_ctx",
    )(*args)


def _fourier_kernel(z_ref, cs_ref, cn_ref, sn_ref, o_ref, a_ref, b_ref, *, scale):
    W = FOURIER_GROUPS * FOURIER_CH

    @pl.when(pl.program_id(1) == 0)
    def _():
        t = _dot(z_ref[0], cs_ref[...])
        a_ref[...] = t[:, :W].astype(BF16)
        b_ref[...] = t[:, W:].astype(BF16)

    y = _dot(cn_ref[...], a_ref[...]) - _dot(sn_ref[...], b_ref[...])
    o_ref[0] = (y * scale).astype(BF16)


def _dft_tables(N):
    n = jnp.arange(N, dtype=jnp.int32)
    ang = ((n[:, None] * n[None, :]) % N).astype(F32) * (2.0 * math.pi / N)
    c = np.arange(FOURIER_CH, dtype=np.int64)
    angc = 2.0 * np.pi * ((c[:, None] * c[None, :]) % FOURIER_CH).astype(np.float64) / FOURIER_CH
    eye = np.eye(FOURIER_GROUPS)
    cs = np.concatenate([np.kron(eye, np.cos(angc)), np.kron(eye, np.sin(angc))], axis=1)
    return jnp.asarray(cs, F32).astype(BF16), jnp.cos(ang).astype(BF16), jnp.sin(ang).astype(BF16)


def _fourier_call(u, dft, tm):
    B, N, _ = u.shape
    W = FOURIER_GROUPS * FOURIER_CH
    cs, cn, sn = dft
    return pl.pallas_call(
        functools.partial(_fourier_kernel, scale=float((N * FOURIER_CH) ** -0.5)),
        grid=(B, N // tm),
        in_specs=[pl.BlockSpec((1, N, W), lambda b, i: (b, 0, U_FZ // 256)),
                  pl.BlockSpec((W, 2 * W), lambda b, i: (0, 0)),
                  pl.BlockSpec((tm, N), lambda b, i: (i, 0)),
                  pl.BlockSpec((tm, N), lambda b, i: (i, 0))],
        out_specs=pl.BlockSpec((1, tm, W), lambda b, i: (b, i, 0)),
        out_shape=jax.ShapeDtypeStruct((B, N, W), BF16),
        scratch_shapes=[pltpu.VMEM((N, W), BF16), pltpu.VMEM((N, W), BF16)],
        compiler_params=_cparams("parallel", "arbitrary"),
        name="fourier",
    )(u, cs, cn, sn)


def _win_kernel(*refs, tq, has_local):
    if has_local:
        sink_ref, q_ref, k_ref, v_ref, kc_ref, vc_ref, o_ref = refs
    else:
        sink_ref, q_ref, kc_ref, vc_ref, o_ref = refs
    g, i = pl.program_id(1), pl.program_id(2)
    q = q_ref[0]
    lane = lax.broadcasted_iota(jnp.int32, (1, 2 * WIN_DH), 1)
    lo = lane < WIN_DH
    zero = jnp.zeros_like(q)
    q2 = jnp.concatenate([jnp.where(lo, q, zero), jnp.where(lo, zero, q)], axis=0)
    row = lax.broadcasted_iota(jnp.int32, (2 * tq, 1), 0)
    sink = jnp.where(row < tq, sink_ref[2 * g], sink_ref[2 * g + 1])
    kc, vc = kc_ref[0], vc_ref[0]
    s_c = _dot_nt(q2, kc)
    m = jnp.maximum(jnp.max(s_c, axis=-1, keepdims=True), sink)
    if has_local:
        N = k_ref.shape[1]
        span = tq + 2 * WINDOW
        start = pl.multiple_of(jnp.clip(i * tq - WINDOW, 0, N - span), WINDOW)
        kw = k_ref[0, pl.ds(start, span), :]
        vw = v_ref[0, pl.ds(start, span), :]
        s_l = _dot_nt(q2, kw)
        qpos = i * tq + jnp.where(row < tq, row, row - tq)
        kpos = start + lax.broadcasted_iota(jnp.int32, (1, span), 1)
        s_l = jnp.where(jnp.abs(kpos - qpos) <= WINDOW, s_l, -jnp.inf)
        m = jnp.maximum(m, jnp.max(s_l, axis=-1, keepdims=True))
    e_c = jnp.exp(s_c - m)
    den = jnp.sum(e_c, axis=-1, keepdims=True) + jnp.exp(sink - m)
    if has_local:
        e_l = jnp.exp(s_l - m)
        den = den + jnp.sum(e_l, axis=-1, keepdims=True)
    inv = 1.0 / den
    o2 = _dot((e_c * inv).astype(BF16), vc)
    if has_local:
        o2 = o2 + _dot((e_l * inv).astype(BF16), vw)
    o_ref[0] = jnp.where(lo, o2[:tq], o2[tq:]).astype(BF16)


def _win_call(u, uc, sink, tq):
    has_local = u is not None
    src = u if has_local else uc
    B, N, _ = src.shape
    C = uc.shape[1]
    W = 2 * WIN_DH
    G = WIN_KV_HEADS
    in_specs = [pl.BlockSpec(memory_space=pltpu.SMEM),
                pl.BlockSpec((1, tq, W), lambda b, g, i: (b, i, U_WQ // W + g))]
    args = [sink, src]
    if has_local:
        in_specs += [pl.BlockSpec((1, N, W), lambda b, g, i: (b, 0, U_WK // W + g)),
                     pl.BlockSpec((1, N, W), lambda b, g, i: (b, 0, U_WV // W + g))]
        args += [u, u]
    in_specs += [pl.BlockSpec((1, C, W), lambda b, g, i: (b, 0, U_WK // W + g)),
                 pl.BlockSpec((1, C, W), lambda b, g, i: (b, 0, U_WV // W + g))]
    args += [uc, uc]
    return pl.pallas_call(
        functools.partial(_win_kernel, tq=tq, has_local=has_local),
        grid=(B, G, N // tq),
        in_specs=in_specs,
        out_specs=pl.BlockSpec((1, tq, W), lambda b, g, i: (b, i, g)),
        out_shape=jax.ShapeDtypeStruct((B, N, G * W), BF16),
        compiler_params=_cparams("parallel", "parallel", "parallel"),
        name="win_attn" if has_local else "win_attn_ctx",
    )(*args)


def _route(sel, score):
    EG = EXPERTS_PER_GROUP
    rows = [sel[e:e + 1] for e in range(N_EXPERTS)]
    srow = [score[e:e + 1] for e in range(N_EXPERTS)]
    best, best_score = None, None
    for g in range(N_EXPERT_GROUPS):
        r = rows[EG * g:EG * (g + 1)]
        top2 = None
        for a in range(EG):
            for b in range(a + 1, EG):
                pair = r[a] + r[b]
                top2 = pair if top2 is None else jnp.maximum(top2, pair)
        if g == 0:
            best, best_score = jnp.zeros(top2.shape, jnp.int32), top2
        else:
            upd = top2 > best_score
            best = jnp.where(upd, g, best)
            best_score = jnp.where(upd, top2, best_score)

    def pick(vals, j):
        out = vals[j]
        for g in range(1, N_EXPERT_GROUPS):
            out = jnp.where(best == g, vals[EG * g + j], out)
        return out

    neg = jnp.full(best_score.shape, -jnp.inf, F32)
    m1, m2 = neg, neg
    i1 = i2 = jnp.zeros(best.shape, jnp.int32)
    w1 = w2 = jnp.zeros(best_score.shape, F32)
    for j in range(EG):
        v, sc = pick(rows, j), pick(srow, j)
        gt1 = v > m1
        gt2 = v > m2
        m2 = jnp.where(gt1, m1, jnp.where(gt2, v, m2))
        i2 = jnp.where(gt1, i1, jnp.where(gt2, j, i2))
        w2 = jnp.where(gt1, w1, jnp.where(gt2, sc, w2))
        m1 = jnp.where(gt1, v, m1)
        i1 = jnp.where(gt1, j, i1)
        w1 = jnp.where(gt1, sc, w1)
    tot = w1 + w2
    return best * EG + i1, best * EG + i2, w1 / tot, w2 / tot


def _outproj_kernel(ya_ref, yb_ref, yc_ref, yd_ref, w_ref, x_ref, g1_ref, sh2_ref, sc2_ref, lg_ref, lb_ref,
                    rw_ref, rb_ref, x1_ref, h2_ref, wcol_ref, e_ref, cnt_ref, seen_ref):
    tn = x_ref.shape[1]
    i = pl.program_id(1)

    @pl.when(i == 0)
    def _():
        seen_ref[...] = jnp.zeros_like(seen_ref)

    y = _dot(ya_ref[0], w_ref[0:256, :])
    y = y + _dot(yb_ref[0], w_ref[256:512, :])
    y = y + _dot(yc_ref[0], w_ref[512:768, :])
    y = y + _dot(yd_ref[0], w_ref[768:1024, :])
    x1 = _ln(ALPHA * x_ref[0] + g1_ref[0] * y) * lg_ref[...] + lb_ref[...]
    x1_ref[0] = x1
    h2 = _ln(x1) * (1.0 + sc2_ref[0]) + sh2_ref[0]
    logits = lax.dot_general(rw_ref[...], h2, (((1,), (1,)), ((), ())), preferred_element_type=F32,
                             precision=lax.Precision.HIGHEST)
    h2_ref[0] = h2
    score = _sigmoid(logits)
    e1, e2, w1, w2 = _route(score + rb_ref[...], score)
    wcol_ref[0] = jnp.concatenate([w1, w2, jnp.zeros((126, tn), F32)], axis=0).T
    erow = lax.broadcasted_iota(jnp.int32, (N_EXPERTS, tn), 0)
    hit1, hit2 = erow == e1, erow == e2
    chosen = jnp.where(hit1, 1.0, 0.0) + jnp.where(hit2, 1.0, 0.0)
    earlier = jnp.where(lax.broadcasted_iota(jnp.int32, (tn, tn), 0) < lax.broadcasted_iota(jnp.int32, (tn, tn), 1),
                        1.0, 0.0).astype(BF16)
    before = _dot(chosen.astype(BF16), earlier) + seen_ref[...]
    r1 = jnp.sum(jnp.where(hit1, before, 0.0), axis=0, keepdims=True)
    r2 = jnp.sum(jnp.where(hit2, before, 0.0), axis=0, keepdims=True)
    e_ref[0] = jnp.concatenate([e1, e2, r1.astype(jnp.int32), r2.astype(jnp.int32), jnp.zeros((4, tn), jnp.int32)],
                               axis=0)
    seen_ref[...] += jnp.sum(chosen, axis=1, keepdims=True)

    @pl.when(i == pl.num_programs(1) - 1)
    def _():
        cnt_ref[0] = jnp.broadcast_to(seen_ref[...], (N_EXPERTS, 128))


def _outproj_call(ys, w_out, x, mod, mod_row, ln_g, ln_b, router_wt, router_bias, tn):
    B, N, D = x.shape
    row = (lambda b: b) if mod_row is None else (lambda b: mod_row)
    yspec = pl.BlockSpec((1, tn, 256), lambda b, i: (b, i, 0))
    mspec = lambda j: pl.BlockSpec((1, 1, D), lambda b, i: (row(b), 0, j))
    vec = pl.BlockSpec((1, D), lambda b, i: (0, 0))
    return pl.pallas_call(
        _outproj_kernel,
        grid=(B, N // tn),
        in_specs=[yspec, yspec, yspec, yspec,
                  pl.BlockSpec((D, D), lambda b, i: (0, 0)),
                  pl.BlockSpec((1, tn, D), lambda b, i: (b, i, 0)),
                  mspec(2), mspec(3), mspec(4), vec, vec,
                  pl.BlockSpec((N_EXPERTS, D), lambda b, i: (0, 0)),
                  pl.BlockSpec((N_EXPERTS, 1), lambda b, i: (0, 0))],
        out_specs=[pl.BlockSpec((1, tn, D), lambda b, i: (b, i, 0)),
                   pl.BlockSpec((1, tn, D), lambda b, i: (b, i, 0)),
                   pl.BlockSpec((1, tn, 128), lambda b, i: (b, i, 0)),
                   pl.BlockSpec((1, 8, tn), lambda b, i: (b, 0, i)),
                   pl.BlockSpec((1, N_EXPERTS, 128), lambda b, i: (b, 0, 0))],
        out_shape=[jax.ShapeDtypeStruct((B, N, D), F32), jax.ShapeDtypeStruct((B, N, D), F32),
                   jax.ShapeDtypeStruct((B, N, 128), F32), jax.ShapeDtypeStruct((B, 8, N), jnp.int32),
                   jax.ShapeDtypeStruct((B, N_EXPERTS, 128), F32)],
        scratch_shapes=[pltpu.VMEM((N_EXPERTS, 1), F32)],
        compiler_params=_cparams("parallel", "arbitrary"),
        name="outproj_router",
    )(*ys, w_out, x, mod, mod, mod, ln_g.reshape(1, D), ln_b.reshape(1, D), router_wt, router_bias.reshape(-1, 1))


MOE_ROWS = 512


def _slot_table(er, cnt, tm, tn):
    B, _, N = er.shape
    E = N_EXPERTS
    c = cnt[:, :, 0].astype(jnp.int32)
    counts = jnp.sum(c, axis=0)
    ptiles = (counts + tm - 1) // tm
    pend = jnp.cumsum(ptiles) * tm
    base = (pend - ptiles * tm)[None, :] + jnp.cumsum(c, axis=0) - c
    eids = jnp.arange(E, dtype=jnp.int32)[None, :, None]
    slots = [jnp.sum(jnp.where(er[:, k, None, :] == eids, base[:, :, None], 0), axis=1) + er[:, 2 + k, :]
             for k in range(2)]
    slots = jnp.stack(slots, axis=1).reshape(B, 2, N // tn, tn).transpose(0, 2, 1, 3).reshape(-1, 2, tn)
    tile0 = jnp.arange((2 * B * N + E * tm) // tm, dtype=jnp.int32) * tm
    tile_e = jnp.minimum(jnp.sum((pend[None, :] <= tile0[:, None]).astype(jnp.int32), axis=1), E - 1)
    n_tiles = jnp.sum(ptiles).astype(jnp.int32).reshape(1)
    return slots.astype(jnp.int32), tile_e, (pend.astype(jnp.int32), ptiles.astype(jnp.int32), n_tiles)


def _dispatch_kernel(pend_ref, ptiles_ref, nt_ref, slot_ref, h_ref, xs_hbm, zbuf, sem, zsem):
    tn = h_ref.shape[1]
    tm = zbuf.shape[0]
    n_max = xs_hbm.shape[0] // tm

    @pl.when((pl.program_id(0) == 0) & (pl.program_id(1) == 0))
    def _():
        zbuf[...] = jnp.zeros_like(zbuf)

        def fill(row0):
            return pltpu.make_async_copy(zbuf, xs_hbm.at[pl.ds(pl.multiple_of(row0, tm), tm)], zsem.at[0])

        def each_fill(act):
            for e in range(N_EXPERTS):
                @pl.when(ptiles_ref[e] > 0)
                def _():
                    act(fill(pend_ref[e] - tm))

                @pl.when(nt_ref[0] + e < n_max)
                def _():
                    act(fill((nt_ref[0] + e) * tm))

        each_fill(lambda copy: copy.start())
        each_fill(lambda copy: copy.wait())

    for r in range(tn):
        for k in range(2):
            pltpu.make_async_copy(h_ref.at[0, pl.ds(r, 1)], xs_hbm.at[pl.ds(slot_ref[0, k, r], 1)], sem.at[0]).start()
    for k in range(2):
        pltpu.make_async_copy(h_ref.at[0], xs_hbm.at[pl.ds(0, tn)], sem.at[0]).wait()


def _dispatch_call(h2, slots, tiles, n_rows, tm, tn):
    B, N, D = h2.shape
    nt = N // tn
    return pl.pallas_call(
        _dispatch_kernel,
        grid_spec=pltpu.PrefetchScalarGridSpec(
            num_scalar_prefetch=3,
            grid=(B, nt),
            in_specs=[pl.BlockSpec((1, 2, tn), lambda b, i, *_: (b * nt + i, 0, 0), memory_space=pltpu.SMEM),
                      pl.BlockSpec((1, tn, D), lambda b, i, *_: (b, i, 0))],
            out_specs=pl.BlockSpec(memory_space=pl.ANY),
            scratch_shapes=[pltpu.VMEM((tm, D), F32), pltpu.SemaphoreType.DMA((1,)), pltpu.SemaphoreType.DMA((1,))]),
        out_shape=jax.ShapeDtypeStruct((n_rows, D), F32),
        compiler_params=_cparams("arbitrary", "arbitrary"),
        name="moe_dispatch",
    )(*tiles, slots, h2)


def _gffn_kernel(te_ref, nt_ref, x_ref, wg_ref, wu_ref, wd_ref, o_ref, wg_bf, wu_bf, wd_bf):
    i = pl.program_id(0)

    @pl.when((i == 0) | (te_ref[i] != te_ref[jnp.maximum(i - 1, 0)]))
    def _():
        wg_bf[...] = wg_ref[0].astype(BF16)
        wu_bf[...] = wu_ref[0].astype(BF16)
        wd_bf[...] = wd_ref[0].astype(BF16)

    @pl.when(i < nt_ref[0])
    def _():
        x = x_ref[...].astype(BF16)
        gate = _dot(x, wg_bf[...])
        up = _dot(x, wu_bf[...])
        act = (gate * _sigmoid(gate) * up).astype(BF16)
        o_ref[...] = _dot(act, wd_bf[...])

    @pl.when(i >= nt_ref[0])
    def _():
        o_ref[...] = jnp.zeros_like(o_ref)


def _gffn_call(xs, tile_e, n_tiles, weights, tm):
    S, D = xs.shape
    layer, wg, wu, wd = weights
    F = wg.shape[-1]
    wspec = lambda shape: pl.BlockSpec((None,) + shape, lambda i, te, nt: (layer, te[i], 0, 0))
    return pl.pallas_call(
        _gffn_kernel,
        grid_spec=pltpu.PrefetchScalarGridSpec(
            num_scalar_prefetch=2,
            grid=(S // tm,),
            in_specs=[pl.BlockSpec((tm, D), lambda i, te, nt: (i, 0)),
                      wspec((1, D, F)), wspec((1, D, F)), wspec((1, F, D))],
            out_specs=pl.BlockSpec((tm, D), lambda i, te, nt: (i, 0)),
            scratch_shapes=[pltpu.VMEM((D, F), BF16), pltpu.VMEM((D, F), BF16), pltpu.VMEM((F, D), BF16)]),
        out_shape=jax.ShapeDtypeStruct((S, D), F32),
        compiler_params=_cparams("arbitrary"),
        name="moe_ffn",
    )(tile_e, n_tiles, xs, wg, wu, wd)


def _moe_out_kernel(slot_ref, slotn_ref, x1_ref, wcol_ref, g2_ref, lg_ref, lb_ref, ys_hbm, o_ref, ybuf, sem):
    t = pl.program_id(0)
    cur = t % 2
    tn = x1_ref.shape[1]

    def issue(s_ref, buf):
        for r in range(tn):
            for k in range(2):
                pltpu.make_async_copy(ys_hbm.at[pl.ds(s_ref[0, k, r], 1)], ybuf.at[buf, k, pl.ds(r, 1)],
                                      sem.at[buf]).start()

    @pl.when(t == 0)
    def _():
        issue(slot_ref, 0)

    @pl.when(t + 1 < pl.num_programs(0))
    def _():
        issue(slotn_ref, 1 - cur)

    for k in range(2):
        pltpu.make_async_copy(ys_hbm.at[pl.ds(0, tn)], ybuf.at[cur, k], sem.at[cur]).wait()
    w = wcol_ref[0]
    moe = w[:, 0:1] * ybuf[cur, 0] + w[:, 1:2] * ybuf[cur, 1]
    o_ref[0] = _ln(ALPHA * x1_ref[0] + g2_ref[0] * moe) * lg_ref[...] + lb_ref[...]


def _moe_out_call(x1, ys, slots, wcol, mod, mod_row, ln_g, ln_b, tn):
    B, N, D = x1.shape
    nt = N // tn
    last = B * nt - 1
    row = (lambda t: t // nt) if mod_row is None else (lambda t: mod_row)
    sspec = lambda f: pl.BlockSpec((1, 2, tn), lambda t: (f(t), 0, 0), memory_space=pltpu.SMEM)
    vec = pl.BlockSpec((1, D), lambda t: (0, 0))
    return pl.pallas_call(
        _moe_out_kernel,
        grid=(B * nt,),
        in_specs=[sspec(lambda t: t), sspec(lambda t: jnp.minimum(t + 1, last)),
                  pl.BlockSpec((1, tn, D), lambda t: (t // nt, t % nt, 0)),
                  pl.BlockSpec((1, tn, 128), lambda t: (t // nt, t % nt, 0)),
                  pl.BlockSpec((1, 1, D), lambda t: (row(t), 0, 5)),
                  vec, vec,
                  pl.BlockSpec(memory_space=pl.ANY)],
        out_specs=pl.BlockSpec((1, tn, D), lambda t: (t // nt, t % nt, 0)),
        out_shape=jax.ShapeDtypeStruct((B, N, D), F32),
        scratch_shapes=[pltpu.VMEM((2, 2, tn, D), F32), pltpu.SemaphoreType.DMA((2,))],
        compiler_params=_cparams("arbitrary"),
        name="moe_out",
    )(slots, slots, x1, wcol, mod, ln_g.reshape(1, D), ln_b.reshape(1, D), ys)


def _moe_sparse(x1, h2, wcol, er, cnt, mod, mod_row, weights, ln_g, ln_b, tn):
    B, N, D = x1.shape
    slots, tile_e, tiles = _slot_table(er, cnt, MOE_ROWS, tn)
    xs = _dispatch_call(h2, slots, tiles, 2 * B * N + N_EXPERTS * MOE_ROWS, MOE_ROWS, tn)
    ys = _gffn_call(xs, tile_e, tiles[2], weights, MOE_ROWS)
    return _moe_out_call(x1, ys, slots, wcol, mod, mod_row, ln_g, ln_b, tn)


def _inproj_weights(w_in_l):
    head, wk, wv = w_in_l[:, :U_WK], w_in_l[:, U_WK:U_WK + 2 * WIN_DH], w_in_l[:, U_WK + 2 * WIN_DH:]
    dup = lambda w: jnp.concatenate([w[:, :WIN_DH], w[:, :WIN_DH], w[:, WIN_DH:], w[:, WIN_DH:]], axis=1)
    return jnp.concatenate([head, dup(wk), dup(wv)], axis=1).astype(BF16)


def _rope_tables(N):
    rows = N // GRID_W
    row_pos = jnp.repeat(jnp.arange(rows, dtype=F32), GRID_W)
    col_pos = jnp.tile(jnp.arange(GRID_W, dtype=F32), rows)
    out = []
    for dh in (DIFF_DH, WIN_DH):
        n_axis = dh // 4
        inv = ROPE_BASE ** (-jnp.arange(n_axis, dtype=F32) / n_axis)
        ang = jnp.concatenate([row_pos[:, None] * inv, col_pos[:, None] * inv], -1)
        cos = jnp.repeat(jnp.cos(ang), 2, axis=-1)
        sin = jnp.repeat(jnp.sin(ang), 2, axis=-1) * jnp.tile(jnp.array([-1.0, 1.0], F32), dh // 2)
        out += [jnp.tile(cos, (1, 256 // dh)), jnp.tile(sin, (1, 256 // dh))]
    return tuple(out)


def _pick_tile(n, pref):
    return pref if n % pref == 0 else n


def kernel(x, c, ctx, c_ctx, w_mod, b_mod, w_in, w_out, conv_w, conv_b, conv_norm_g, conv_norm_b, diff_lambda,
           diff_subln_g, win_sink, ln_mix_g, ln_mix_b, ln_ffn_g, ln_ffn_b, router_w, router_bias, exp_w_gate,
           exp_w_up, exp_w_down):
    B, N, D = x.shape
    C = ctx.shape[1]
    ctx_row = B
    pad_rows = (-(B + 1)) % 8
    cs = jnp.concatenate([c, c_ctx[None, :], jnp.zeros((pad_rows, D), F32)], axis=0)
    mod_all = _mod_call(cs, w_mod, b_mod)
    tables = _rope_tables(N)
    dft = {n: _dft_tables(n) for n in {N, C}}
    router_wt = router_w.T
    xc = ctx
    for l in range(DEPTH):
        need_ctx = l < DEPTH - 1
        mod = mod_all[l].reshape(-1, 1, 6 * D)
        lam_init = 0.8 - 0.6 * math.exp(-0.3 * l)
        w_ext = _inproj_weights(w_in[l])
        w_o = w_out[l].astype(BF16)
        experts = (l, exp_w_gate, exp_w_up, exp_w_down)

        u = _inproj_call(x, mod, None, w_ext, tables, _pick_tile(N, 512))
        uc = _inproj_call(xc, mod, ctx_row, w_ext, None, _pick_tile(C, 256))

        def mixers(u_lat):
            src = u_lat if u_lat is not None else uc
            n = src.shape[1]
            return (_conv_call(src, conv_w[l], conv_b[l], conv_norm_g[l], conv_norm_b[l]),
                    _diff_call(u_lat, uc, diff_lambda[l], diff_subln_g[l], lam_init, _pick_tile(n, 512)),
                    _fourier_call(src, dft[n], _pick_tile(n, 512)),
                    _win_call(u_lat, uc, win_sink[l], _pick_tile(n, 256)))

        routed = _outproj_call(mixers(u), w_o, x, mod, None, ln_mix_g[l], ln_mix_b[l], router_wt, router_bias,
                               _pick_tile(N, 512))
        x = _moe_sparse(*routed, mod, None, experts, ln_ffn_g[l], ln_ffn_b[l], _pick_tile(N, 512))
        if need_ctx:
            routed = _outproj_call(mixers(None), w_o, xc, mod, ctx_row, ln_mix_g[l], ln_mix_b[l], router_wt,
                                   router_bias, _pick_tile(C, 256))
            xc = _moe_sparse(*routed, mod, ctx_row, experts, ln_ffn_g[l], ln_ffn_b[l], _pick_tile(C, 256))
    return x
```

```python
import functools
import math

import numpy as np
import jax
import jax.numpy as jnp
from jax import lax
from jax.experimental import pallas as pl
from jax.experimental.pallas import tpu as pltpu

F32 = jnp.float32
BF16 = jnp.bfloat16

D_MODEL = 1024
DEPTH = 2
GRID_W = 64
CONV_CH = 256
CONV_WIDTH = 31
DIFF_HEADS = 4
DIFF_DH = 32
FOURIER_GROUPS = 4
FOURIER_CH = 64
WIN_DH = 64
WIN_KV_HEADS = 2
WINDOW = 128
ROPE_BASE = 10000.0
N_EXPERTS = 16
N_EXPERT_GROUPS = 4
EXPERTS_PER_GROUP = 4
D_EXPERT = 512
ALPHA = (2 * DEPTH) ** 0.25
LN_EPS = 1e-5

U_AV, U_AG, U_DQ, U_DK, U_DV, U_FZ, U_WQ, U_WK, U_WV = (256 * i for i in range(9))
U_WIDTH = 9 * 256
ROPE_GROUPS = (U_DQ, U_DK, U_WQ, U_WK)
V7X_VMEM_LIMIT = 48 * 1024 * 1024


def _cparams(*sem):
    return pltpu.CompilerParams(dimension_semantics=sem, vmem_limit_bytes=V7X_VMEM_LIMIT)


def _ln(x):
    mu = jnp.mean(x, axis=-1, keepdims=True)
    xc = x - mu
    var = jnp.mean(xc * xc, axis=-1, keepdims=True)
    return xc * lax.rsqrt(var + LN_EPS)


def _sigmoid(x):
    return 1.0 / (1.0 + jnp.exp(-x))


def _dot(a, b):
    return jnp.dot(a, b, preferred_element_type=F32)


def _dot_nt(a, b):
    return lax.dot_general(a, b, (((1,), (1,)), ((), ())), preferred_element_type=F32)


def _mod_kernel(c_ref, w_ref, b_ref, o_ref):
    c = c_ref[...]
    s = c * _sigmoid(c)
    o_ref[0] = jnp.dot(s, w_ref[0], preferred_element_type=F32, precision=lax.Precision.HIGHEST) + b_ref[0]


def _mod_call(cs, w_mod, b_mod):
    R, D = cs.shape
    return pl.pallas_call(
        _mod_kernel,
        grid=(DEPTH, 6),
        in_specs=[pl.BlockSpec((R, D), lambda l, j: (0, 0)),
                  pl.BlockSpec((1, D, D), lambda l, j: (l, 0, j)),
                  pl.BlockSpec((1, 1, D), lambda l, j: (l, 0, j))],
        out_specs=pl.BlockSpec((1, R, D), lambda l, j: (l, 0, j)),
        out_shape=jax.ShapeDtypeStruct((DEPTH, R, 6 * D), F32),
        compiler_params=_cparams("arbitrary", "arbitrary"),
        name="mod",
    )(cs, w_mod, b_mod.reshape(DEPTH, 1, 6 * D))


def _inproj_kernel(*refs, rope):
    if rope:
        x_ref, sh_ref, sc_ref, w_ref, cd_ref, sd_ref, cw_ref, sw_ref, o_ref = refs
    else:
        x_ref, sh_ref, sc_ref, w_ref, o_ref = refs
    h = (_ln(x_ref[0]) * (1.0 + sc_ref[0]) + sh_ref[0]).astype(BF16)
    q_scale = {U_DQ: DIFF_DH ** -0.5 * math.log2(math.e), U_WQ: WIN_DH ** -0.5}
    even = lax.broadcasted_iota(jnp.int32, (1, 128), 1) % 2 == 0
    for a in range(0, U_WIDTH, 256):
        val = _dot(h, w_ref[:, a:a + 256])
        if a in q_scale:
            val = val * q_scale[a]
        if rope and a in ROPE_GROUPS:
            cos, sin = (cd_ref, sd_ref) if a in (U_DQ, U_DK) else (cw_ref, sw_ref)
            for half in range(2):
                v = val[:, 128 * half:128 * (half + 1)]
                partner = jnp.where(even, pltpu.roll(v, 127, axis=1), pltpu.roll(v, 1, axis=1))
                o_ref[0, :, a + 128 * half:a + 128 * (half + 1)] = (
                    v * cos[:, 128 * half:128 * (half + 1)] + partner * sin[:, 128 * half:128 * (half + 1)]).astype(BF16)
        else:
            o_ref[0, :, a:a + 256] = val.astype(BF16)


def _inproj_call(x, mod, mod_row, w, tables, tn):
    B, N, D = x.shape
    rope = tables is not None
    row = (lambda b: b) if mod_row is None else (lambda b: mod_row)
    in_specs = [pl.BlockSpec((1, tn, D), lambda b, i: (b, i, 0)),
                pl.BlockSpec((1, 1, D), lambda b, i: (row(b), 0, 0)),
                pl.BlockSpec((1, 1, D), lambda b, i: (row(b), 0, 1)),
                pl.BlockSpec(w.shape, lambda b, i: (0, 0))]
    args = [x, mod, mod, w]
    if rope:
        in_specs += [pl.BlockSpec((tn, 256), lambda b, i: (i, 0))] * 4
        args += list(tables)
    return pl.pallas_call(
        functools.partial(_inproj_kernel, rope=rope),
        grid=(B, N // tn),
        in_specs=in_specs,
        out_specs=pl.BlockSpec((1, tn, U_WIDTH), lambda b, i: (b, i, 0)),
        out_shape=jax.ShapeDtypeStruct((B, N, U_WIDTH), BF16),
        compiler_params=_cparams("parallel", "parallel"),
        name="inproj_rope" if rope else "inproj",
    )(*args)


CONV_ROWS = 128
CONV_PAD = 16


def _conv_kernel(u_ref, w_ref, b_ref, g_ref, nb_ref, o_ref, pad_ref, sh_ref):
    N = u_ref.shape[1]
    L = N + 2 * CONV_PAD
    val = u_ref[0, :, 0:CONV_CH].astype(F32)
    gate = u_ref[0, :, CONV_CH:2 * CONV_CH].astype(F32)
    pad_ref[0:CONV_PAD, :] = jnp.zeros((CONV_PAD, CONV_CH), F32)
    pad_ref[CONV_PAD + N:L + 8, :] = jnp.zeros((CONV_PAD + 8, CONV_CH), F32)
    pad_ref[CONV_PAD:CONV_PAD + N, :] = val * _sigmoid(gate)
    for s in range(8):
        sh_ref[s] = pad_ref[s:s + L, :]

    def tile(i, carry):
        r0 = pl.multiple_of(i * CONV_ROWS, CONV_ROWS)
        acc = jnp.zeros((CONV_ROWS, CONV_CH), F32)
        for k in range(CONV_WIDTH):
            off = CONV_PAD - CONV_WIDTH // 2 + k
            acc = acc + w_ref[k:k + 1, :] * sh_ref[off % 8, pl.ds(r0 + 8 * (off // 8), CONV_ROWS), :]
        y = _ln(acc + b_ref[...]) * g_ref[...] + nb_ref[...]
        o_ref[0, pl.ds(r0, CONV_ROWS), :] = (y * _sigmoid(y)).astype(BF16)
        return carry

    lax.fori_loop(0, N // CONV_ROWS, tile, 0)


def _conv_call(u, conv_w, conv_b, conv_ng, conv_nb):
    B, N, _ = u.shape
    vec = pl.BlockSpec((1, CONV_CH), lambda b: (0, 0))
    return pl.pallas_call(
        _conv_kernel,
        grid=(B,),
        in_specs=[pl.BlockSpec((1, N, 2 * CONV_CH), lambda b: (b, 0, 0)),
                  pl.BlockSpec((CONV_WIDTH, CONV_CH), lambda b: (0, 0)), vec, vec, vec],
        out_specs=pl.BlockSpec((1, N, CONV_CH), lambda b: (b, 0, 0)),
        out_shape=jax.ShapeDtypeStruct((B, N, CONV_CH), BF16),
        scratch_shapes=[pltpu.VMEM((N + 2 * CONV_PAD + 8, CONV_CH), F32),
                        pltpu.VMEM((8, N + 2 * CONV_PAD, CONV_CH), F32)],
        compiler_params=_cparams("parallel"),
        name="conv",
    )(u, conv_w, conv_b.reshape(1, -1), conv_ng.reshape(1, -1), conv_nb.reshape(1, -1))


def _diff_kernel(*refs, lam_init, has_lat):
    if has_lat:
        q_ref, k_ref, v_ref, kc_ref, vc_ref, dl_ref, g_ref, o_ref, kall_ref, vt_ref = refs
    else:
        q_ref, kc_ref, vc_ref, dl_ref, g_ref, o_ref, kall_ref, vt_ref = refs
    C = kc_ref.shape[1]
    DV = 2 * DIFF_DH

    @pl.when(pl.program_id(1) == 0)
    def _():
        kall_ref[0:C, :] = kc_ref[0]
        vt_ref[:, 0:C] = vc_ref[0].astype(F32).T.astype(BF16)
        if has_lat:
            N = k_ref.shape[1]
            kall_ref[C:C + N, :] = k_ref[0]
            vt_ref[:, C:C + N] = v_ref[0].astype(F32).T.astype(BF16)

    q = q_ref[0]
    kall = kall_ref[...]
    dl = dl_ref[...]
    lam = (jnp.exp(jnp.sum(dl[0:1] * dl[1:2], axis=-1, keepdims=True))
           - jnp.exp(jnp.sum(dl[2:3] * dl[3:4], axis=-1, keepdims=True)) + lam_init)
    lane = lax.broadcasted_iota(jnp.int32, (1, DIFF_HEADS * DV), 1)
    outs = []
    for h in range(DIFF_HEADS):
        parts = []
        for c in range(2):
            lo = (2 * h + c) * DIFF_DH
            km = jnp.where((lane >= lo) & (lane < lo + DIFF_DH), kall, jnp.zeros_like(kall))
            st = _dot_nt(km, q)
            e = jnp.exp2(st - jnp.max(st, axis=0, keepdims=True))
            parts.append((e, jnp.sum(e, axis=0, keepdims=True)))
        (e0, l0), (e1, l1) = parts
        at = (e0 - (lam * l0 / l1) * e1).astype(BF16)
        ot = _dot(vt_ref[DV * h:DV * (h + 1), :], at) * (1.0 / l0)
        ms = jnp.mean(ot * ot, axis=0, keepdims=True)
        outs.append(ot * lax.rsqrt(ms + LN_EPS))
    yt = jnp.concatenate(outs, axis=0) * g_ref[...] * (1.0 - lam_init)
    o_ref[0] = yt.T.astype(BF16)


def _diff_call(u, uc, diff_lambda, subln_g, lam_init, tq):
    has_lat = u is not None
    src = u if has_lat else uc
    B, N, _ = src.shape
    C = uc.shape[1]
    W = DIFF_HEADS * 2 * DIFF_DH
    n_keys = C + N if has_lat else C
    blk = lambda col: U_DQ // 256 + col
    in_specs = [pl.BlockSpec((1, tq, W), lambda b, i: (b, i, blk(0)))]
    args = [src]
    if has_lat:
        in_specs += [pl.BlockSpec((1, N, W), lambda b, i: (b, 0, blk(1))),
                     pl.BlockSpec((1, N, W), lambda b, i: (b, 0, blk(2)))]
        args += [u, u]
    in_specs += [pl.BlockSpec((1, C, W), lambda b, i: (b, 0, blk(1))),
                 pl.BlockSpec((1, C, W), lambda b, i: (b, 0, blk(2))),
                 pl.BlockSpec((4, DIFF_DH), lambda b, i: (0, 0)),
                 pl.BlockSpec((W, 1), lambda b, i: (0, 0))]
    args += [uc, uc, diff_lambda, jnp.tile(subln_g, DIFF_HEADS).reshape(W, 1)]
    return pl.pallas_call(
        functools.partial(_diff_kernel, lam_init=lam_init, has_lat=has_lat),
        grid=(B, N // tq),
        in_specs=in_specs,
        out_specs=pl.BlockSpec((1, tq, W), lambda b, i: (b, i, 0)),
        out_shape=jax.ShapeDtypeStruct((B, N, W), BF16),
        scratch_shapes=[pltpu.VMEM((n_keys, W), BF16), pltpu.VMEM((W, n_keys), BF16)],
        compiler_params=_cparams("parallel", "arbitrary"),
        name="diff_attn" if has_lat else "diff_attn_ctx",
    )(*args)


def _fourier_kernel(z_ref, cs_ref, cn_ref, sn_ref, o_ref, a_ref, b_ref, *, scale):
    W = FOURIER_GROUPS * FOURIER_CH

    @pl.when(pl.program_id(1) == 0)
    def _():
        t = _dot(z_ref[0], cs_ref[...])
        a_ref[...] = t[:, :W].astype(BF16)
        b_ref[...] = t[:, W:].astype(BF16)

    y = _dot(cn_ref[...], a_ref[...]) - _dot(sn_ref[...], b_ref[...])
    o_ref[0] = (y * scale).astype(BF16)


def _dft_tables(N):
    n = jnp.arange(N, dtype=jnp.int32)
    ang = ((n[:, None] * n[None, :]) % N).astype(F32) * (2.0 * math.pi / N)
    c = np.arange(FOURIER_CH, dtype=np.int64)
    angc = 2.0 * np.pi * ((c[:, None] * c[None, :]) % FOURIER_CH).astype(np.float64) / FOURIER_CH
    eye = np.eye(FOURIER_GROUPS)
    cs = np.concatenate([np.kron(eye, np.cos(angc)), np.kron(eye, np.sin(angc))], axis=1)
    return jnp.asarray(cs, F32).astype(BF16), jnp.cos(ang).astype(BF16), jnp.sin(ang).astype(BF16)


def _fourier_call(u, dft, tm):
    B, N, _ = u.shape
    W = FOURIER_GROUPS * FOURIER_CH
    cs, cn, sn = dft
    return pl.pallas_call(
        functools.partial(_fourier_kernel, scale=float((N * FOURIER_CH) ** -0.5)),
        grid=(B, N // tm),
        in_specs=[pl.BlockSpec((1, N, W), lambda b, i: (b, 0, U_FZ // 256)),
                  pl.BlockSpec((W, 2 * W), lambda b, i: (0, 0)),
                  pl.BlockSpec((tm, N), lambda b, i: (i, 0)),
                  pl.BlockSpec((tm, N), lambda b, i: (i, 0))],
        out_specs=pl.BlockSpec((1, tm, W), lambda b, i: (b, i, 0)),
        out_shape=jax.ShapeDtypeStruct((B, N, W), BF16),
        scratch_shapes=[pltpu.VMEM((N, W), BF16), pltpu.VMEM((N, W), BF16)],
        compiler_params=_cparams("parallel", "arbitrary"),
        name="fourier",
    )(u, cs, cn, sn)


def _win_kernel(*refs, tq, has_local):
    if has_local:
        sink_ref, q_ref, k_ref, v_ref, kc_ref, vc_ref, o_ref = refs
    else:
        sink_ref, q_ref, kc_ref, vc_ref, o_ref = refs
    g, i = pl.program_id(1), pl.program_id(2)
    q = q_ref[0]
    lane = lax.broadcasted_iota(jnp.int32, (1, 2 * WIN_DH), 1)
    lo = lane < WIN_DH
    zero = jnp.zeros_like(q)
    q2 = jnp.concatenate([jnp.where(lo, q, zero), jnp.where(lo, zero, q)], axis=0)
    row = lax.broadcasted_iota(jnp.int32, (2 * tq, 1), 0)
    sink = jnp.where(row < tq, sink_ref[2 * g], sink_ref[2 * g + 1])
    kc, vc = kc_ref[0], vc_ref[0]
    s_c = _dot_nt(q2, kc)
    m = jnp.maximum(jnp.max(s_c, axis=-1, keepdims=True), sink)
    if has_local:
        N = k_ref.shape[1]
        span = tq + 2 * WINDOW
        start = pl.multiple_of(jnp.clip(i * tq - WINDOW, 0, N - span), WINDOW)
        kw = k_ref[0, pl.ds(start, span), :]
        vw = v_ref[0, pl.ds(start, span), :]
        s_l = _dot_nt(q2, kw)
        qpos = i * tq + jnp.where(row < tq, row, row - tq)
        kpos = start + lax.broadcasted_iota(jnp.int32, (1, span), 1)
        s_l = jnp.where(jnp.abs(kpos - qpos) <= WINDOW, s_l, -jnp.inf)
        m = jnp.maximum(m, jnp.max(s_l, axis=-1, keepdims=True))
    e_c = jnp.exp(s_c - m)
    den = jnp.sum(e_c, axis=-1, keepdims=True) + jnp.exp(sink - m)
    if has_local:
        e_l = jnp.exp(s_l - m)
        den = den + jnp.sum(e_l, axis=-1, keepdims=True)
    inv = 1.0 / den
    o2 = _dot((e_c * inv).astype(BF16), vc)
    if has_local:
        o2 = o2 + _dot((e_l * inv).astype(BF16), vw)
    o_ref[0] = jnp.where(lo, o2[:tq], o2[tq:]).astype(BF16)


def _win_call(u, uc, sink, tq):
    has_local = u is not None
    src = u if has_local else uc
    B, N, _ = src.shape
    C = uc.shape[1]
    W = 2 * WIN_DH
    G = WIN_KV_HEADS
    in_specs = [pl.BlockSpec(memory_space=pltpu.SMEM),
                pl.BlockSpec((1, tq, W), lambda b, g, i: (b, i, U_WQ // W + g))]
    args = [sink, src]
    if has_local:
        in_specs += [pl.BlockSpec((1, N, W), lambda b, g, i: (b, 0, U_WK // W + g)),
                     pl.BlockSpec((1, N, W), lambda b, g, i: (b, 0, U_WV // W + g))]
        args += [u, u]
    in_specs += [pl.BlockSpec((1, C, W), lambda b, g, i: (b, 0, U_WK // W + g)),
                 pl.BlockSpec((1, C, W), lambda b, g, i: (b, 0, U_WV // W + g))]
    args += [uc, uc]
    return pl.pallas_call(
        functools.partial(_win_kernel, tq=tq, has_local=has_local),
        grid=(B, G, N // tq),
        in_specs=in_specs,
        out_specs=pl.BlockSpec((1, tq, W), lambda b, g, i: (b, i, g)),
        out_shape=jax.ShapeDtypeStruct((B, N, G * W), BF16),
        compiler_params=_cparams("parallel", "parallel", "parallel"),
        name="win_attn" if has_local else "win_attn_ctx",
    )(*args)


def _route(sel, score):
    EG = EXPERTS_PER_GROUP
    rows = [sel[e:e + 1] for e in range(N_EXPERTS)]
    srow = [score[e:e + 1] for e in range(N_EXPERTS)]
    best, best_score = None, None
    for g in range(N_EXPERT_GROUPS):
        r = rows[EG * g:EG * (g + 1)]
        top2 = None
        for a in range(EG):
            for b in range(a + 1, EG):
                pair = r[a] + r[b]
                top2 = pair if top2 is None else jnp.maximum(top2, pair)
        if g == 0:
            best, best_score = jnp.zeros(top2.shape, jnp.int32), top2
        else:
            upd = top2 > best_score
            best = jnp.where(upd, g, best)
            best_score = jnp.where(upd, top2, best_score)

    def pick(vals, j):
        out = vals[j]
        for g in range(1, N_EXPERT_GROUPS):
            out = jnp.where(best == g, vals[EG * g + j], out)
        return out

    neg = jnp.full(best_score.shape, -jnp.inf, F32)
    m1, m2 = neg, neg
    i1 = i2 = jnp.zeros(best.shape, jnp.int32)
    w1 = w2 = jnp.zeros(best_score.shape, F32)
    for j in range(EG):
        v, sc = pick(rows, j), pick(srow, j)
        gt1 = v > m1
        gt2 = v > m2
        m2 = jnp.where(gt1, m1, jnp.where(gt2, v, m2))
        i2 = jnp.where(gt1, i1, jnp.where(gt2, j, i2))
        w2 = jnp.where(gt1, w1, jnp.where(gt2, sc, w2))
        m1 = jnp.where(gt1, v, m1)
        i1 = jnp.where(gt1, j, i1)
        w1 = jnp.where(gt1, sc, w1)
    tot = w1 + w2
    return best * EG + i1, best * EG + i2, w1 / tot, w2 / tot


def _outproj_kernel(ya_ref, yb_ref, yc_ref, yd_ref, w_ref, x_ref, g1_ref, sh2_ref, sc2_ref, lg_ref, lb_ref,
                    rw_ref, rb_ref, x1_ref, h2_ref, wcol_ref, e_ref, cnt_ref, seen_ref):
    tn = x_ref.shape[1]
    i = pl.program_id(1)

    @pl.when(i == 0)
    def _():
        seen_ref[...] = jnp.zeros_like(seen_ref)

    y = _dot(ya_ref[0], w_ref[0:256, :])
    y = y + _dot(yb_ref[0], w_ref[256:512, :])
    y = y + _dot(yc_ref[0], w_ref[512:768, :])
    y = y + _dot(yd_ref[0], w_ref[768:1024, :])
    x1 = _ln(ALPHA * x_ref[0] + g1_ref[0] * y) * lg_ref[...] + lb_ref[...]
    x1_ref[0] = x1
    h2 = _ln(x1) * (1.0 + sc2_ref[0]) + sh2_ref[0]
    logits = lax.dot_general(rw_ref[...], h2, (((1,), (1,)), ((), ())), preferred_element_type=F32,
                             precision=lax.Precision.HIGHEST)
    h2_ref[0] = h2
    score = _sigmoid(logits)
    e1, e2, w1, w2 = _route(score + rb_ref[...], score)
    wcol_ref[0] = jnp.concatenate([w1, w2, jnp.zeros((126, tn), F32)], axis=0).T
    erow = lax.broadcasted_iota(jnp.int32, (N_EXPERTS, tn), 0)
    hit1, hit2 = erow == e1, erow == e2
    chosen = jnp.where(hit1, 1.0, 0.0) + jnp.where(hit2, 1.0, 0.0)
    earlier = jnp.where(lax.broadcasted_iota(jnp.int32, (tn, tn), 0) < lax.broadcasted_iota(jnp.int32, (tn, tn), 1),
                        1.0, 0.0).astype(BF16)
    before = _dot(chosen.astype(BF16), earlier) + seen_ref[...]
    r1 = jnp.sum(jnp.where(hit1, before, 0.0), axis=0, keepdims=True)
    r2 = jnp.sum(jnp.where(hit2, before, 0.0), axis=0, keepdims=True)
    e_ref[0] = jnp.concatenate([e1, e2, r1.astype(jnp.int32), r2.astype(jnp.int32), jnp.zeros((4, tn), jnp.int32)],
                               axis=0)
    seen_ref[...] += jnp.sum(chosen, axis=1, keepdims=True)

    @pl.when(i == pl.num_programs(1) - 1)
    def _():
        cnt_ref[0] = jnp.broadcast_to(seen_ref[...], (N_EXPERTS, 128))


def _outproj_call(ys, w_out, x, mod, mod_row, ln_g, ln_b, router_wt, router_bias, tn):
    B, N, D = x.shape
    row = (lambda b: b) if mod_row is None else (lambda b: mod_row)
    yspec = pl.BlockSpec((1, tn, 256), lambda b, i: (b, i, 0))
    mspec = lambda j: pl.BlockSpec((1, 1, D), lambda b, i: (row(b), 0, j))
    vec = pl.BlockSpec((1, D), lambda b, i: (0, 0))
    return pl.pallas_call(
        _outproj_kernel,
        grid=(B, N // tn),
        in_specs=[yspec, yspec, yspec, yspec,
                  pl.BlockSpec((D, D), lambda b, i: (0, 0)),
                  pl.BlockSpec((1, tn, D), lambda b, i: (b, i, 0)),
                  mspec(2), mspec(3), mspec(4), vec, vec,
                  pl.BlockSpec((N_EXPERTS, D), lambda b, i: (0, 0)),
                  pl.BlockSpec((N_EXPERTS, 1), lambda b, i: (0, 0))],
        out_specs=[pl.BlockSpec((1, tn, D), lambda b, i: (b, i, 0)),
                   pl.BlockSpec((1, tn, D), lambda b, i: (b, i, 0)),
                   pl.BlockSpec((1, tn, 128), lambda b, i: (b, i, 0)),
                   pl.BlockSpec((1, 8, tn), lambda b, i: (b, 0, i)),
                   pl.BlockSpec((1, N_EXPERTS, 128), lambda b, i: (b, 0, 0))],
        out_shape=[jax.ShapeDtypeStruct((B, N, D), F32), jax.ShapeDtypeStruct((B, N, D), F32),
                   jax.ShapeDtypeStruct((B, N, 128), F32), jax.ShapeDtypeStruct((B, 8, N), jnp.int32),
                   jax.ShapeDtypeStruct((B, N_EXPERTS, 128), F32)],
        scratch_shapes=[pltpu.VMEM((N_EXPERTS, 1), F32)],
        compiler_params=_cparams("parallel", "arbitrary"),
        name="outproj_router",
    )(*ys, w_out, x, mod, mod, mod, ln_g.reshape(1, D), ln_b.reshape(1, D), router_wt, router_bias.reshape(-1, 1))


MOE_ROWS = 512


def _slot_table(er, cnt, tm, tn):
    B, _, N = er.shape
    E = N_EXPERTS
    c = cnt[:, :, 0].astype(jnp.int32)
    counts = jnp.sum(c, axis=0)
    ptiles = (counts + tm - 1) // tm
    pend = jnp.cumsum(ptiles) * tm
    base = (pend - ptiles * tm)[None, :] + jnp.cumsum(c, axis=0) - c
    eids = jnp.arange(E, dtype=jnp.int32)[None, :, None]
    slots = [jnp.sum(jnp.where(er[:, k, None, :] == eids, base[:, :, None], 0), axis=1) + er[:, 2 + k, :]
             for k in range(2)]
    slots = jnp.stack(slots, axis=1).reshape(B, 2, N // tn, tn).transpose(0, 2, 1, 3).reshape(-1, 2, tn)
    tile0 = jnp.arange((2 * B * N + E * tm) // tm, dtype=jnp.int32) * tm
    tile_e = jnp.minimum(jnp.sum((pend[None, :] <= tile0[:, None]).astype(jnp.int32), axis=1), E - 1)
    n_tiles = jnp.sum(ptiles).astype(jnp.int32).reshape(1)
    return slots.astype(jnp.int32), tile_e, (pend.astype(jnp.int32), ptiles.astype(jnp.int32), n_tiles)


def _dispatch_kernel(pend_ref, ptiles_ref, nt_ref, slot_ref, h_ref, xs_hbm, zbuf, sem, zsem):
    tn = h_ref.shape[1]
    tm = zbuf.shape[0]
    n_max = xs_hbm.shape[0] // tm

    @pl.when((pl.program_id(0) == 0) & (pl.program_id(1) == 0))
    def _():
        zbuf[...] = jnp.zeros_like(zbuf)

        def fill(row0):
            return pltpu.make_async_copy(zbuf, xs_hbm.at[pl.ds(pl.multiple_of(row0, tm), tm)], zsem.at[0])

        def each_fill(act):
            for e in range(N_EXPERTS):
                @pl.when(ptiles_ref[e] > 0)
                def _():
                    act(fill(pend_ref[e] - tm))

                @pl.when(nt_ref[0] + e < n_max)
                def _():
                    act(fill((nt_ref[0] + e) * tm))

        each_fill(lambda copy: copy.start())
        each_fill(lambda copy: copy.wait())

    for r in range(tn):
        for k in range(2):
            pltpu.make_async_copy(h_ref.at[0, pl.ds(r, 1)], xs_hbm.at[pl.ds(slot_ref[0, k, r], 1)],
                                  sem.at[0]).start(priority=k)
    for k in range(2):
        pltpu.make_async_copy(h_ref.at[0], xs_hbm.at[pl.ds(0, tn)], sem.at[0]).wait()


def _dispatch_call(h2, slots, tiles, n_rows, tm, tn):
    B, N, D = h2.shape
    nt = N // tn
    return pl.pallas_call(
        _dispatch_kernel,
        grid_spec=pltpu.PrefetchScalarGridSpec(
            num_scalar_prefetch=3,
            grid=(B, nt),
            in_specs=[pl.BlockSpec((1, 2, tn), lambda b, i, *_: (b * nt + i, 0, 0), memory_space=pltpu.SMEM),
                      pl.BlockSpec((1, tn, D), lambda b, i, *_: (b, i, 0))],
            out_specs=pl.BlockSpec(memory_space=pl.ANY),
            scratch_shapes=[pltpu.VMEM((tm, D), F32), pltpu.SemaphoreType.DMA((1,)), pltpu.SemaphoreType.DMA((1,))]),
        out_shape=jax.ShapeDtypeStruct((n_rows, D), F32),
        compiler_params=_cparams("arbitrary", "arbitrary"),
        name="moe_dispatch",
    )(*tiles, slots, h2)


def _gffn_kernel(te_ref, nt_ref, x_ref, wg_ref, wu_ref, wd_ref, o_ref, wg_bf, wu_bf, wd_bf):
    i = pl.program_id(0)

    @pl.when((i == 0) | (te_ref[i] != te_ref[jnp.maximum(i - 1, 0)]))
    def _():
        wg_bf[...] = wg_ref[0].astype(BF16)
        wu_bf[...] = wu_ref[0].astype(BF16)
        wd_bf[...] = wd_ref[0].astype(BF16)

    @pl.when(i < nt_ref[0])
    def _():
        x = x_ref[...].astype(BF16)
        gate = _dot(x, wg_bf[...])
        up = _dot(x, wu_bf[...])
        act = (gate * _sigmoid(gate) * up).astype(BF16)
        o_ref[...] = _dot(act, wd_bf[...])

    @pl.when(i >= nt_ref[0])
    def _():
        o_ref[...] = jnp.zeros_like(o_ref)


def _gffn_call(xs, tile_e, n_tiles, weights, tm):
    S, D = xs.shape
    layer, wg, wu, wd = weights
    F = wg.shape[-1]
    wspec = lambda shape: pl.BlockSpec((None,) + shape, lambda i, te, nt: (layer, te[i], 0, 0))
    return pl.pallas_call(
        _gffn_kernel,
        grid_spec=pltpu.PrefetchScalarGridSpec(
            num_scalar_prefetch=2,
            grid=(S // tm,),
            in_specs=[pl.BlockSpec((tm, D), lambda i, te, nt: (i, 0)),
                      wspec((1, D, F)), wspec((1, D, F)), wspec((1, F, D))],
            out_specs=pl.BlockSpec((tm, D), lambda i, te, nt: (i, 0)),
            scratch_shapes=[pltpu.VMEM((D, F), BF16), pltpu.VMEM((D, F), BF16), pltpu.VMEM((F, D), BF16)]),
        out_shape=jax.ShapeDtypeStruct((S, D), F32),
        compiler_params=_cparams("arbitrary"),
        name="moe_ffn",
    )(tile_e, n_tiles, xs, wg, wu, wd)


def _moe_out_kernel(slot_ref, slotn_ref, x1_ref, wcol_ref, g2_ref, lg_ref, lb_ref, ys_hbm, o_ref, ybuf, sem):
    t = pl.program_id(0)
    cur = t % 2
    tn = x1_ref.shape[1]

    def issue(s_ref, buf):
        for r in range(tn):
            for k in range(2):
                pltpu.make_async_copy(ys_hbm.at[pl.ds(s_ref[0, k, r], 1)], ybuf.at[buf, k, pl.ds(r, 1)],
                                      sem.at[buf]).start(priority=k)

    @pl.when(t == 0)
    def _():
        issue(slot_ref, 0)

    @pl.when(t + 1 < pl.num_programs(0))
    def _():
        issue(slotn_ref, 1 - cur)

    for k in range(2):
        pltpu.make_async_copy(ys_hbm.at[pl.ds(0, tn)], ybuf.at[cur, k], sem.at[cur]).wait()
    w = wcol_ref[0]
    moe = w[:, 0:1] * ybuf[cur, 0] + w[:, 1:2] * ybuf[cur, 1]
    o_ref[0] = _ln(ALPHA * x1_ref[0] + g2_ref[0] * moe) * lg_ref[...] + lb_ref[...]


def _moe_out_call(x1, ys, slots, wcol, mod, mod_row, ln_g, ln_b, tn):
    B, N, D = x1.shape
    nt = N // tn
    last = B * nt - 1
    row = (lambda t: t // nt) if mod_row is None else (lambda t: mod_row)
    sspec = lambda f: pl.BlockSpec((1, 2, tn), lambda t: (f(t), 0, 0), memory_space=pltpu.SMEM)
    vec = pl.BlockSpec((1, D), lambda t: (0, 0))
    return pl.pallas_call(
        _moe_out_kernel,
        grid=(B * nt,),
        in_specs=[sspec(lambda t: t), sspec(lambda t: jnp.minimum(t + 1, last)),
                  pl.BlockSpec((1, tn, D), lambda t: (t // nt, t % nt, 0)),
                  pl.BlockSpec((1, tn, 128), lambda t: (t // nt, t % nt, 0)),
                  pl.BlockSpec((1, 1, D), lambda t: (row(t), 0, 5)),
                  vec, vec,
                  pl.BlockSpec(memory_space=pl.ANY)],
        out_specs=pl.BlockSpec((1, tn, D), lambda t: (t // nt, t % nt, 0)),
        out_shape=jax.ShapeDtypeStruct((B, N, D), F32),
        scratch_shapes=[pltpu.VMEM((2, 2, tn, D), F32), pltpu.SemaphoreType.DMA((2,))],
        compiler_params=_cparams("arbitrary"),
        name="moe_out",
    )(slots, slots, x1, wcol, mod, ln_g.reshape(1, D), ln_b.reshape(1, D), ys)


def _moe_sparse(x1, h2, wcol, er, cnt, mod, mod_row, weights, ln_g, ln_b, tn):
    B, N, D = x1.shape
    slots, tile_e, tiles = _slot_table(er, cnt, MOE_ROWS, tn)
    xs = _dispatch_call(h2, slots, tiles, 2 * B * N + N_EXPERTS * MOE_ROWS, MOE_ROWS, tn)
    ys = _gffn_call(xs, tile_e, tiles[2], weights, MOE_ROWS)
    return _moe_out_call(x1, ys, slots, wcol, mod, mod_row, ln_g, ln_b, tn)


def _inproj_weights(w_in_l):
    head, wk, wv = w_in_l[:, :U_WK], w_in_l[:, U_WK:U_WK + 2 * WIN_DH], w_in_l[:, U_WK + 2 * WIN_DH:]
    dup = lambda w: jnp.concatenate([w[:, :WIN_DH], w[:, :WIN_DH], w[:, WIN_DH:], w[:, WIN_DH:]], axis=1)
    return jnp.concatenate([head, dup(wk), dup(wv)], axis=1).astype(BF16)


def _rope_tables(N):
    rows = N // GRID_W
    row_pos = jnp.repeat(jnp.arange(rows, dtype=F32), GRID_W)
    col_pos = jnp.tile(jnp.arange(GRID_W, dtype=F32), rows)
    out = []
    for dh in (DIFF_DH, WIN_DH):
        n_axis = dh // 4
        inv = ROPE_BASE ** (-jnp.arange(n_axis, dtype=F32) / n_axis)
        ang = jnp.concatenate([row_pos[:, None] * inv, col_pos[:, None] * inv], -1)
        cos = jnp.repeat(jnp.cos(ang), 2, axis=-1)
        sin = jnp.repeat(jnp.sin(ang), 2, axis=-1) * jnp.tile(jnp.array([-1.0, 1.0], F32), dh // 2)
        out += [jnp.tile(cos, (1, 256 // dh)), jnp.tile(sin, (1, 256 // dh))]
    return tuple(out)


def _pick_tile(n, pref):
    return pref if n % pref == 0 else n


def kernel(x, c, ctx, c_ctx, w_mod, b_mod, w_in, w_out, conv_w, conv_b, conv_norm_g, conv_norm_b, diff_lambda,
           diff_subln_g, win_sink, ln_mix_g, ln_mix_b, ln_ffn_g, ln_ffn_b, router_w, router_bias, exp_w_gate,
           exp_w_up, exp_w_down):
    B, N, D = x.shape
    C = ctx.shape[1]
    ctx_row = B
    pad_rows = (-(B + 1)) % 8
    cs = jnp.concatenate([c, c_ctx[None, :], jnp.zeros((pad_rows, D), F32)], axis=0)
    mod_all = _mod_call(cs, w_mod, b_mod)
    tables = _rope_tables(N)
    dft = {n: _dft_tables(n) for n in {N, C}}
    router_wt = router_w.T
    xc = ctx
    for l in range(DEPTH):
        need_ctx = l < DEPTH - 1
        mod = mod_all[l].reshape(-1, 1, 6 * D)
        lam_init = 0.8 - 0.6 * math.exp(-0.3 * l)
        w_ext = _inproj_weights(w_in[l])
        w_o = w_out[l].astype(BF16)
        experts = (l, exp_w_gate, exp_w_up, exp_w_down)

        u = _inproj_call(x, mod, None, w_ext, tables, _pick_tile(N, 1024))
        uc = _inproj_call(xc, mod, ctx_row, w_ext, None, _pick_tile(C, 256))

        def mixers(u_lat):
            src = u_lat if u_lat is not None else uc
            n = src.shape[1]
            return (_conv_call(src, conv_w[l], conv_b[l], conv_norm_g[l], conv_norm_b[l]),
                    _diff_call(u_lat, uc, diff_lambda[l], diff_subln_g[l], lam_init, _pick_tile(n, 512)),
                    _fourier_call(src, dft[n], _pick_tile(n, 512)),
                    _win_call(u_lat, uc, win_sink[l], _pick_tile(n, 256)))

        routed = _outproj_call(mixers(u), w_o, x, mod, None, ln_mix_g[l], ln_mix_b[l], router_wt, router_bias,
                               _pick_tile(N, 1024))
        x = _moe_sparse(*routed, mod, None, experts, ln_ffn_g[l], ln_ffn_b[l], _pick_tile(N, 512))
        if need_ctx:
            routed = _outproj_call(mixers(None), w_o, xc, mod, ctx_row, ln_mix_g[l], ln_mix_b[l], router_wt,
                                   router_bias, _pick_tile(C, 256))
            xc = _moe_sparse(*routed, mod, ctx_row, experts, ln_ffn_g[l], ln_ffn_b[l], _pick_tile(C, 256))
    return x
```

```python
import functools
import math

import numpy as np
import jax
import jax.numpy as jnp
from jax import lax
from jax.experimental import pallas as pl
from jax.experimental.pallas import tpu as pltpu

F32 = jnp.float32
BF16 = jnp.bfloat16

D_MODEL = 1024
DEPTH = 2
GRID_W = 64
CONV_CH = 256
CONV_WIDTH = 31
DIFF_HEADS = 4
DIFF_DH = 32
FOURIER_GROUPS = 4
FOURIER_CH = 64
WIN_DH = 64
WIN_KV_HEADS = 2
WINDOW = 128
ROPE_BASE = 10000.0
N_EXPERTS = 16
N_EXPERT_GROUPS = 4
EXPERTS_PER_GROUP = 4
D_EXPERT = 512
ALPHA = (2 * DEPTH) ** 0.25
LN_EPS = 1e-5

U_AV, U_AG, U_DQ, U_DK, U_DV, U_FZ, U_WQ, U_WK, U_WV = (256 * i for i in range(9))
U_WIDTH = 9 * 256
ROPE_GROUPS = (U_DQ, U_DK, U_WQ, U_WK)
V7X_VMEM_LIMIT = 48 * 1024 * 1024


def _cparams(*sem):
    return pltpu.CompilerParams(dimension_semantics=sem, vmem_limit_bytes=V7X_VMEM_LIMIT)


def _ln(x):
    mu = jnp.mean(x, axis=-1, keepdims=True)
    xc = x - mu
    var = jnp.mean(xc * xc, axis=-1, keepdims=True)
    return xc * lax.rsqrt(var + LN_EPS)


def _sigmoid(x):
    return 1.0 / (1.0 + jnp.exp(-x))


def _dot(a, b):
    return jnp.dot(a, b, preferred_element_type=F32)


def _dot_nt(a, b):
    return lax.dot_general(a, b, (((1,), (1,)), ((), ())), preferred_element_type=F32)


def _mod_kernel(c_ref, w_ref, b_ref, o_ref):
    c = c_ref[...]
    s = c * _sigmoid(c)
    o_ref[0] = jnp.dot(s, w_ref[0], preferred_element_type=F32, precision=lax.Precision.HIGHEST) + b_ref[0]


def _mod_call(cs, w_mod, b_mod):
    R, D = cs.shape
    return pl.pallas_call(
        _mod_kernel,
        grid=(DEPTH, 6),
        in_specs=[pl.BlockSpec((R, D), lambda l, j: (0, 0)),
                  pl.BlockSpec((1, D, D), lambda l, j: (l, 0, j)),
                  pl.BlockSpec((1, 1, D), lambda l, j: (l, 0, j))],
        out_specs=pl.BlockSpec((1, R, D), lambda l, j: (l, 0, j)),
        out_shape=jax.ShapeDtypeStruct((DEPTH, R, 6 * D), F32),
        compiler_params=_cparams("arbitrary", "arbitrary"),
        name="mod",
    )(cs, w_mod, b_mod.reshape(DEPTH, 1, 6 * D))


def _inproj_kernel(*refs, rope):
    if rope:
        x_ref, sh_ref, sc_ref, w_ref, cd_ref, sd_ref, cw_ref, sw_ref, o_ref = refs
    else:
        x_ref, sh_ref, sc_ref, w_ref, o_ref = refs
    h = (_ln(x_ref[0]) * (1.0 + sc_ref[0]) + sh_ref[0]).astype(BF16)
    q_scale = {U_DQ: DIFF_DH ** -0.5 * math.log2(math.e), U_WQ: WIN_DH ** -0.5}
    even = lax.broadcasted_iota(jnp.int32, (1, 128), 1) % 2 == 0
    for a in range(0, U_WIDTH, 256):
        val = _dot(h, w_ref[:, a:a + 256])
        if a in q_scale:
            val = val * q_scale[a]
        if rope and a in ROPE_GROUPS:
            cos, sin = (cd_ref, sd_ref) if a in (U_DQ, U_DK) else (cw_ref, sw_ref)
            for half in range(2):
                v = val[:, 128 * half:128 * (half + 1)]
                partner = jnp.where(even, pltpu.roll(v, 127, axis=1), pltpu.roll(v, 1, axis=1))
                o_ref[0, :, a + 128 * half:a + 128 * (half + 1)] = (
                    v * cos[:, 128 * half:128 * (half + 1)] + partner * sin[:, 128 * half:128 * (half + 1)]).astype(BF16)
        else:
            o_ref[0, :, a:a + 256] = val.astype(BF16)


def _inproj_call(x, mod, mod_row, w, tables, tn):
    B, N, D = x.shape
    rope = tables is not None
    row = (lambda b: b) if mod_row is None else (lambda b: mod_row)
    in_specs = [pl.BlockSpec((1, tn, D), lambda b, i: (b, i, 0)),
                pl.BlockSpec((1, 1, D), lambda b, i: (row(b), 0, 0)),
                pl.BlockSpec((1, 1, D), lambda b, i: (row(b), 0, 1)),
                pl.BlockSpec(w.shape, lambda b, i: (0, 0))]
    args = [x, mod, mod, w]
    if rope:
        in_specs += [pl.BlockSpec((tn, 256), lambda b, i: (i, 0))] * 4
        args += list(tables)
    return pl.pallas_call(
        functools.partial(_inproj_kernel, rope=rope),
        grid=(B, N // tn),
        in_specs=in_specs,
        out_specs=pl.BlockSpec((1, tn, U_WIDTH), lambda b, i: (b, i, 0)),
        out_shape=jax.ShapeDtypeStruct((B, N, U_WIDTH), BF16),
        compiler_params=_cparams("parallel", "parallel"),
        name="inproj_rope" if rope else "inproj",
    )(*args)


CONV_ROWS = 128
CONV_PAD = 16


def _conv_kernel(u_ref, w_ref, b_ref, g_ref, nb_ref, o_ref, pad_ref, sh_ref):
    N = u_ref.shape[1]
    L = N + 2 * CONV_PAD
    val = u_ref[0, :, 0:CONV_CH].astype(F32)
    gate = u_ref[0, :, CONV_CH:2 * CONV_CH].astype(F32)
    pad_ref[0:CONV_PAD, :] = jnp.zeros((CONV_PAD, CONV_CH), F32)
    pad_ref[CONV_PAD + N:L + 8, :] = jnp.zeros((CONV_PAD + 8, CONV_CH), F32)
    pad_ref[CONV_PAD:CONV_PAD + N, :] = val * _sigmoid(gate)
    for s in range(8):
        sh_ref[s] = pad_ref[s:s + L, :]

    def tile(i, carry):
        r0 = pl.multiple_of(i * CONV_ROWS, CONV_ROWS)
        acc = jnp.zeros((CONV_ROWS, CONV_CH), F32)
        for k in range(CONV_WIDTH):
            off = CONV_PAD - CONV_WIDTH // 2 + k
            acc = acc + w_ref[k:k + 1, :] * sh_ref[off % 8, pl.ds(r0 + 8 * (off // 8), CONV_ROWS), :]
        y = _ln(acc + b_ref[...]) * g_ref[...] + nb_ref[...]
        o_ref[0, pl.ds(r0, CONV_ROWS), :] = (y * _sigmoid(y)).astype(BF16)
        return carry

    lax.fori_loop(0, N // CONV_ROWS, tile, 0)


def _conv_call(u, conv_w, conv_b, conv_ng, conv_nb):
    B, N, _ = u.shape
    vec = pl.BlockSpec((1, CONV_CH), lambda b: (0, 0))
    return pl.pallas_call(
        _conv_kernel,
        grid=(B,),
        in_specs=[pl.BlockSpec((1, N, 2 * CONV_CH), lambda b: (b, 0, 0)),
                  pl.BlockSpec((CONV_WIDTH, CONV_CH), lambda b: (0, 0)), vec, vec, vec],
        out_specs=pl.BlockSpec((1, N, CONV_CH), lambda b: (b, 0, 0)),
        out_shape=jax.ShapeDtypeStruct((B, N, CONV_CH), BF16),
        scratch_shapes=[pltpu.VMEM((N + 2 * CONV_PAD + 8, CONV_CH), F32),
                        pltpu.VMEM((8, N + 2 * CONV_PAD, CONV_CH), F32)],
        compiler_params=_cparams("parallel"),
        name="conv",
    )(u, conv_w, conv_b.reshape(1, -1), conv_ng.reshape(1, -1), conv_nb.reshape(1, -1))


def _diff_kernel(*refs, lam_init, has_lat):
    if has_lat:
        q_ref, k_ref, v_ref, kc_ref, vc_ref, dl_ref, g_ref, o_ref, kall_ref, vt_ref = refs
    else:
        q_ref, kc_ref, vc_ref, dl_ref, g_ref, o_ref, kall_ref, vt_ref = refs
    C = kc_ref.shape[1]
    DV = 2 * DIFF_DH

    @pl.when(pl.program_id(1) == 0)
    def _():
        kall_ref[0:C, :] = kc_ref[0]
        vt_ref[:, 0:C] = vc_ref[0].astype(F32).T.astype(BF16)
        if has_lat:
            N = k_ref.shape[1]
            kall_ref[C:C + N, :] = k_ref[0]
            vt_ref[:, C:C + N] = v_ref[0].astype(F32).T.astype(BF16)

    q = q_ref[0]
    kall = kall_ref[...]
    dl = dl_ref[...]
    lam = (jnp.exp(jnp.sum(dl[0:1] * dl[1:2], axis=-1, keepdims=True))
           - jnp.exp(jnp.sum(dl[2:3] * dl[3:4], axis=-1, keepdims=True)) + lam_init)
    lane = lax.broadcasted_iota(jnp.int32, (1, DIFF_HEADS * DV), 1)
    outs = []
    for h in range(DIFF_HEADS):
        parts = []
        for c in range(2):
            lo = (2 * h + c) * DIFF_DH
            km = jnp.where((lane >= lo) & (lane < lo + DIFF_DH), kall, jnp.zeros_like(kall))
            st = _dot_nt(km, q)
            e = jnp.exp2(st - jnp.max(st, axis=0, keepdims=True))
            parts.append((e, jnp.sum(e, axis=0, keepdims=True)))
        (e0, l0), (e1, l1) = parts
        at = (e0 - (lam * l0 / l1) * e1).astype(BF16)
        ot = _dot(vt_ref[DV * h:DV * (h + 1), :], at) * (1.0 / l0)
        ms = jnp.mean(ot * ot, axis=0, keepdims=True)
        outs.append(ot * lax.rsqrt(ms + LN_EPS))
    yt = jnp.concatenate(outs, axis=0) * g_ref[...] * (1.0 - lam_init)
    o_ref[0] = yt.T.astype(BF16)


def _diff_call(u, uc, diff_lambda, subln_g, lam_init, tq):
    has_lat = u is not None
    src = u if has_lat else uc
    B, N, _ = src.shape
    C = uc.shape[1]
    W = DIFF_HEADS * 2 * DIFF_DH
    n_keys = C + N if has_lat else C
    blk = lambda col: U_DQ // 256 + col
    in_specs = [pl.BlockSpec((1, tq, W), lambda b, i: (b, i, blk(0)))]
    args = [src]
    if has_lat:
        in_specs += [pl.BlockSpec((1, N, W), lambda b, i: (b, 0, blk(1))),
                     pl.BlockSpec((1, N, W), lambda b, i: (b, 0, blk(2)))]
        args += [u, u]
    in_specs += [pl.BlockSpec((1, C, W), lambda b, i: (b, 0, blk(1))),
                 pl.BlockSpec((1, C, W), lambda b, i: (b, 0, blk(2))),
                 pl.BlockSpec((4, DIFF_DH), lambda b, i: (0, 0)),
                 pl.BlockSpec((W, 1), lambda b, i: (0, 0))]
    args += [uc, uc, diff_lambda, jnp.tile(subln_g, DIFF_HEADS).reshape(W, 1)]
    return pl.pallas_call(
        functools.partial(_diff_kernel, lam_init=lam_init, has_lat=has_lat),
        grid=(B, N // tq),
        in_specs=in_specs,
        out_specs=pl.BlockSpec((1, tq, W), lambda b, i: (b, i, 0)),
        out_shape=jax.ShapeDtypeStruct((B, N, W), BF16),
        scratch_shapes=[pltpu.VMEM((n_keys, W), BF16), pltpu.VMEM((W, n_keys), BF16)],
        compiler_params=_cparams("parallel", "arbitrary"),
        name="diff_attn" if has_lat else "diff_attn_ctx",
    )(*args)


def _fourier_kernel(z_ref, cs_ref, cn_ref, sn_ref, o_ref, a_ref, b_ref, *, scale):
    W = FOURIER_GROUPS * FOURIER_CH

    @pl.when(pl.program_id(1) == 0)
    def _():
        t = _dot(z_ref[0], cs_ref[...])
        a_ref[...] = t[:, :W].astype(BF16)
        b_ref[...] = t[:, W:].astype(BF16)

    y = _dot(cn_ref[...], a_ref[...]) - _dot(sn_ref[...], b_ref[...])
    o_ref[0] = (y * scale).astype(BF16)


def _dft_tables(N):
    n = jnp.arange(N, dtype=jnp.int32)
    ang = ((n[:, None] * n[None, :]) % N).astype(F32) * (2.0 * math.pi / N)
    c = np.arange(FOURIER_CH, dtype=np.int64)
    angc = 2.0 * np.pi * ((c[:, None] * c[None, :]) % FOURIER_CH).astype(np.float64) / FOURIER_CH
    eye = np.eye(FOURIER_GROUPS)
    cs = np.concatenate([np.kron(eye, np.cos(angc)), np.kron(eye, np.sin(angc))], axis=1)
    return jnp.asarray(cs, F32).astype(BF16), jnp.cos(ang).astype(BF16), jnp.sin(ang).astype(BF16)


def _fourier_call(u, dft, tm):
    B, N, _ = u.shape
    W = FOURIER_GROUPS * FOURIER_CH
    cs, cn, sn = dft
    return pl.pallas_call(
        functools.partial(_fourier_kernel, scale=float((N * FOURIER_CH) ** -0.5)),
        grid=(B, N // tm),
        in_specs=[pl.BlockSpec((1, N, W), lambda b, i: (b, 0, U_FZ // 256)),
                  pl.BlockSpec((W, 2 * W), lambda b, i: (0, 0)),
                  pl.BlockSpec((tm, N), lambda b, i: (i, 0)),
                  pl.BlockSpec((tm, N), lambda b, i: (i, 0))],
        out_specs=pl.BlockSpec((1, tm, W), lambda b, i: (b, i, 0)),
        out_shape=jax.ShapeDtypeStruct((B, N, W), BF16),
        scratch_shapes=[pltpu.VMEM((N, W), BF16), pltpu.VMEM((N, W), BF16)],
        compiler_params=_cparams("parallel", "arbitrary"),
        name="fourier",
    )(u, cs, cn, sn)


def _win_kernel(*refs, tq, has_local):
    if has_local:
        sink_ref, q_ref, k_ref, v_ref, kc_ref, vc_ref, o_ref = refs
    else:
        sink_ref, q_ref, kc_ref, vc_ref, o_ref = refs
    g, i = pl.program_id(1), pl.program_id(2)
    q = q_ref[0]
    lane = lax.broadcasted_iota(jnp.int32, (1, 2 * WIN_DH), 1)
    lo = lane < WIN_DH
    zero = jnp.zeros_like(q)
    q2 = jnp.concatenate([jnp.where(lo, q, zero), jnp.where(lo, zero, q)], axis=0)
    row = lax.broadcasted_iota(jnp.int32, (2 * tq, 1), 0)
    sink = jnp.where(row < tq, sink_ref[2 * g], sink_ref[2 * g + 1])
    kc, vc = kc_ref[0], vc_ref[0]
    s_c = _dot_nt(q2, kc)
    m = jnp.maximum(jnp.max(s_c, axis=-1, keepdims=True), sink)
    if has_local:
        N = k_ref.shape[1]
        span = tq + 2 * WINDOW
        start = pl.multiple_of(jnp.clip(i * tq - WINDOW, 0, N - span), WINDOW)
        kw = k_ref[0, pl.ds(start, span), :]
        vw = v_ref[0, pl.ds(start, span), :]
        s_l = _dot_nt(q2, kw)
        qpos = i * tq + jnp.where(row < tq, row, row - tq)
        kpos = start + lax.broadcasted_iota(jnp.int32, (1, span), 1)
        s_l = jnp.where(jnp.abs(kpos - qpos) <= WINDOW, s_l, -jnp.inf)
        m = jnp.maximum(m, jnp.max(s_l, axis=-1, keepdims=True))
    e_c = jnp.exp(s_c - m)
    den = jnp.sum(e_c, axis=-1, keepdims=True) + jnp.exp(sink - m)
    if has_local:
        e_l = jnp.exp(s_l - m)
        den = den + jnp.sum(e_l, axis=-1, keepdims=True)
    inv = 1.0 / den
    o2 = _dot((e_c * inv).astype(BF16), vc)
    if has_local:
        o2 = o2 + _dot((e_l * inv).astype(BF16), vw)
    o_ref[0] = jnp.where(lo, o2[:tq], o2[tq:]).astype(BF16)


def _win_call(u, uc, sink, tq):
    has_local = u is not None
    src = u if has_local else uc
    B, N, _ = src.shape
    C = uc.shape[1]
    W = 2 * WIN_DH
    G = WIN_KV_HEADS
    in_specs = [pl.BlockSpec(memory_space=pltpu.SMEM),
                pl.BlockSpec((1, tq, W), lambda b, g, i: (b, i, U_WQ // W + g))]
    args = [sink, src]
    if has_local:
        in_specs += [pl.BlockSpec((1, N, W), lambda b, g, i: (b, 0, U_WK // W + g)),
                     pl.BlockSpec((1, N, W), lambda b, g, i: (b, 0, U_WV // W + g))]
        args += [u, u]
    in_specs += [pl.BlockSpec((1, C, W), lambda b, g, i: (b, 0, U_WK // W + g)),
                 pl.BlockSpec((1, C, W), lambda b, g, i: (b, 0, U_WV // W + g))]
    args += [uc, uc]
    return pl.pallas_call(
        functools.partial(_win_kernel, tq=tq, has_local=has_local),
        grid=(B, G, N // tq),
        in_specs=in_specs,
        out_specs=pl.BlockSpec((1, tq, W), lambda b, g, i: (b, i, g)),
        out_shape=jax.ShapeDtypeStruct((B, N, G * W), BF16),
        compiler_params=_cparams("parallel", "parallel", "parallel"),
        name="win_attn" if has_local else "win_attn_ctx",
    )(*args)


def _route(sel, score):
    EG = EXPERTS_PER_GROUP
    rows = [sel[e:e + 1] for e in range(N_EXPERTS)]
    srow = [score[e:e + 1] for e in range(N_EXPERTS)]
    best, best_score = None, None
    for g in range(N_EXPERT_GROUPS):
        r = rows[EG * g:EG * (g + 1)]
        top2 = None
        for a in range(EG):
            for b in range(a + 1, EG):
                pair = r[a] + r[b]
                top2 = pair if top2 is None else jnp.maximum(top2, pair)
        if g == 0:
            best, best_score = jnp.zeros(top2.shape, jnp.int32), top2
        else:
            upd = top2 > best_score
            best = jnp.where(upd, g, best)
            best_score = jnp.where(upd, top2, best_score)

    def pick(vals, j):
        out = vals[j]
        for g in range(1, N_EXPERT_GROUPS):
            out = jnp.where(best == g, vals[EG * g + j], out)
        return out

    neg = jnp.full(best_score.shape, -jnp.inf, F32)
    m1, m2 = neg, neg
    i1 = i2 = jnp.zeros(best.shape, jnp.int32)
    w1 = w2 = jnp.zeros(best_score.shape, F32)
    for j in range(EG):
        v, sc = pick(rows, j), pick(srow, j)
        gt1 = v > m1
        gt2 = v > m2
        m2 = jnp.where(gt1, m1, jnp.where(gt2, v, m2))
        i2 = jnp.where(gt1, i1, jnp.where(gt2, j, i2))
        w2 = jnp.where(gt1, w1, jnp.where(gt2, sc, w2))
        m1 = jnp.where(gt1, v, m1)
        i1 = jnp.where(gt1, j, i1)
        w1 = jnp.where(gt1, sc, w1)
    tot = w1 + w2
    return best * EG + i1, best * EG + i2, w1 / tot, w2 / tot


def _outproj_kernel(ya_ref, yb_ref, yc_ref, yd_ref, w_ref, x_ref, g1_ref, sh2_ref, sc2_ref, lg_ref, lb_ref,
                    rw_ref, rb_ref, x1_ref, h2_ref, wcol_ref, e_ref, cnt_ref, seen_ref):
    tn = x_ref.shape[1]
    i = pl.program_id(1)

    @pl.when(i == 0)
    def _():
        seen_ref[...] = jnp.zeros_like(seen_ref)

    y = _dot(ya_ref[0], w_ref[0:256, :])
    y = y + _dot(yb_ref[0], w_ref[256:512, :])
    y = y + _dot(yc_ref[0], w_ref[512:768, :])
    y = y + _dot(yd_ref[0], w_ref[768:1024, :])
    x1 = _ln(ALPHA * x_ref[0] + g1_ref[0] * y) * lg_ref[...] + lb_ref[...]
    x1_ref[0] = x1
    h2 = _ln(x1) * (1.0 + sc2_ref[0]) + sh2_ref[0]
    logits = lax.dot_general(rw_ref[...], h2, (((1,), (1,)), ((), ())), preferred_element_type=F32,
                             precision=lax.Precision.HIGHEST)
    h2_ref[0] = h2
    score = _sigmoid(logits)
    e1, e2, w1, w2 = _route(score + rb_ref[...], score)
    wcol_ref[0] = jnp.concatenate([w1, w2, jnp.zeros((126, tn), F32)], axis=0).T
    erow = lax.broadcasted_iota(jnp.int32, (N_EXPERTS, tn), 0)
    hit1, hit2 = erow == e1, erow == e2
    chosen = jnp.where(hit1, 1.0, 0.0) + jnp.where(hit2, 1.0, 0.0)
    earlier = jnp.where(lax.broadcasted_iota(jnp.int32, (tn, tn), 0) < lax.broadcasted_iota(jnp.int32, (tn, tn), 1),
                        1.0, 0.0).astype(BF16)
    before = _dot(chosen.astype(BF16), earlier) + seen_ref[...]
    r1 = jnp.sum(jnp.where(hit1, before, 0.0), axis=0, keepdims=True)
    r2 = jnp.sum(jnp.where(hit2, before, 0.0), axis=0, keepdims=True)
    e_ref[0] = jnp.concatenate([e1, e2, r1.astype(jnp.int32), r2.astype(jnp.int32), jnp.zeros((4, tn), jnp.int32)],
                               axis=0)
    seen_ref[...] += jnp.sum(chosen, axis=1, keepdims=True)

    @pl.when(i == pl.num_programs(1) - 1)
    def _():
        cnt_ref[0] = jnp.broadcast_to(seen_ref[...], (N_EXPERTS, 128))


def _outproj_call(ys, w_out, x, mod, mod_row, ln_g, ln_b, router_wt, router_bias, tn):
    B, N, D = x.shape
    row = (lambda b: b) if mod_row is None else (lambda b: mod_row)
    yspec = pl.BlockSpec((1, tn, 256), lambda b, i: (b, i, 0))
    mspec = lambda j: pl.BlockSpec((1, 1, D), lambda b, i: (row(b), 0, j))
    vec = pl.BlockSpec((1, D), lambda b, i: (0, 0))
    return pl.pallas_call(
        _outproj_kernel,
        grid=(B, N // tn),
        in_specs=[yspec, yspec, yspec, yspec,
                  pl.BlockSpec((D, D), lambda b, i: (0, 0)),
                  pl.BlockSpec((1, tn, D), lambda b, i: (b, i, 0)),
                  mspec(2), mspec(3), mspec(4), vec, vec,
                  pl.BlockSpec((N_EXPERTS, D), lambda b, i: (0, 0)),
                  pl.BlockSpec((N_EXPERTS, 1), lambda b, i: (0, 0))],
        out_specs=[pl.BlockSpec((1, tn, D), lambda b, i: (b, i, 0)),
                   pl.BlockSpec((1, tn, D), lambda b, i: (b, i, 0)),
                   pl.BlockSpec((1, tn, 128), lambda b, i: (b, i, 0)),
                   pl.BlockSpec((1, 8, tn), lambda b, i: (b, 0, i)),
                   pl.BlockSpec((1, N_EXPERTS, 128), lambda b, i: (b, 0, 0))],
        out_shape=[jax.ShapeDtypeStruct((B, N, D), F32), jax.ShapeDtypeStruct((B, N, D), F32),
                   jax.ShapeDtypeStruct((B, N, 128), F32), jax.ShapeDtypeStruct((B, 8, N), jnp.int32),
                   jax.ShapeDtypeStruct((B, N_EXPERTS, 128), F32)],
        scratch_shapes=[pltpu.VMEM((N_EXPERTS, 1), F32)],
        compiler_params=_cparams("parallel", "arbitrary"),
        name="outproj_router",
    )(*ys, w_out, x, mod, mod, mod, ln_g.reshape(1, D), ln_b.reshape(1, D), router_wt, router_bias.reshape(-1, 1))


MOE_ROWS = 512


def _slot_blocks(slots, tn):
    B, _, N = slots.shape
    return slots.reshape(B, 2, N // tn, tn).transpose(0, 2, 1, 3).reshape(-1, 2, tn)


def _slot_table(er, cnt, tm):
    B, _, N = er.shape
    E = N_EXPERTS
    c = cnt[:, :, 0].astype(jnp.int32)
    counts = jnp.sum(c, axis=0)
    ptiles = (counts + tm - 1) // tm
    pend = jnp.cumsum(ptiles) * tm
    base = (pend - ptiles * tm)[None, :] + jnp.cumsum(c, axis=0) - c
    eids = jnp.arange(E, dtype=jnp.int32)[None, :, None]
    slots = [jnp.sum(jnp.where(er[:, k, None, :] == eids, base[:, :, None], 0), axis=1) + er[:, 2 + k, :]
             for k in range(2)]
    slots = jnp.stack(slots, axis=1)
    tile0 = jnp.arange((2 * B * N + E * tm) // tm, dtype=jnp.int32) * tm
    tile_e = jnp.minimum(jnp.sum((pend[None, :] <= tile0[:, None]).astype(jnp.int32), axis=1), E - 1)
    n_tiles = jnp.sum(ptiles).astype(jnp.int32).reshape(1)
    return slots.astype(jnp.int32), tile_e, (pend.astype(jnp.int32), ptiles.astype(jnp.int32), n_tiles)


def _dispatch_kernel(pend_ref, ptiles_ref, nt_ref, slot_ref, h_ref, xs_hbm, zbuf, sem, zsem):
    tn = h_ref.shape[1]
    tm = zbuf.shape[0]
    n_max = xs_hbm.shape[0] // tm

    @pl.when((pl.program_id(0) == 0) & (pl.program_id(1) == 0))
    def _():
        zbuf[...] = jnp.zeros_like(zbuf)

        def fill(row0):
            return pltpu.make_async_copy(zbuf, xs_hbm.at[pl.ds(pl.multiple_of(row0, tm), tm)], zsem.at[0])

        def each_fill(act):
            for e in range(N_EXPERTS):
                @pl.when(ptiles_ref[e] > 0)
                def _():
                    act(fill(pend_ref[e] - tm))

                @pl.when(nt_ref[0] + e < n_max)
                def _():
                    act(fill((nt_ref[0] + e) * tm))

        each_fill(lambda copy: copy.start())
        each_fill(lambda copy: copy.wait())

    for r in range(tn):
        for k in range(2):
            pltpu.make_async_copy(h_ref.at[0, pl.ds(r, 1)], xs_hbm.at[pl.ds(slot_ref[0, k, r], 1)],
                                  sem.at[0]).start(priority=k)
    for k in range(2):
        pltpu.make_async_copy(h_ref.at[0], xs_hbm.at[pl.ds(0, tn)], sem.at[0]).wait()


def _dispatch_call(h2, slots, tiles, n_rows, tm, tn):
    B, N, D = h2.shape
    nt = N // tn
    return pl.pallas_call(
        _dispatch_kernel,
        grid_spec=pltpu.PrefetchScalarGridSpec(
            num_scalar_prefetch=3,
            grid=(B, nt),
            in_specs=[pl.BlockSpec((1, 2, tn), lambda b, i, *_: (b * nt + i, 0, 0), memory_space=pltpu.SMEM),
                      pl.BlockSpec((1, tn, D), lambda b, i, *_: (b, i, 0))],
            out_specs=pl.BlockSpec(memory_space=pl.ANY),
            scratch_shapes=[pltpu.VMEM((tm, D), F32), pltpu.SemaphoreType.DMA((1,)), pltpu.SemaphoreType.DMA((1,))]),
        out_shape=jax.ShapeDtypeStruct((n_rows, D), F32),
        compiler_params=_cparams("arbitrary", "arbitrary"),
        name="moe_dispatch",
    )(*tiles, slots, h2)


def _gffn_kernel(te_ref, nt_ref, x_ref, wg_ref, wu_ref, wd_ref, o_ref, wg_bf, wu_bf, wd_bf):
    i = pl.program_id(0)

    @pl.when((i == 0) | (te_ref[i] != te_ref[jnp.maximum(i - 1, 0)]))
    def _():
        wg_bf[...] = wg_ref[0].astype(BF16)
        wu_bf[...] = wu_ref[0].astype(BF16)
        wd_bf[...] = wd_ref[0].astype(BF16)

    @pl.when(i < nt_ref[0])
    def _():
        x = x_ref[...].astype(BF16)
        gate = _dot(x, wg_bf[...])
        up = _dot(x, wu_bf[...])
        act = (gate * _sigmoid(gate) * up).astype(BF16)
        o_ref[...] = _dot(act, wd_bf[...])

    @pl.when(i >= nt_ref[0])
    def _():
        o_ref[...] = jnp.zeros_like(o_ref)


def _gffn_call(xs, tile_e, n_tiles, weights, tm):
    S, D = xs.shape
    layer, wg, wu, wd = weights
    F = wg.shape[-1]
    wspec = lambda shape: pl.BlockSpec((None,) + shape, lambda i, te, nt: (layer, te[i], 0, 0))
    return pl.pallas_call(
        _gffn_kernel,
        grid_spec=pltpu.PrefetchScalarGridSpec(
            num_scalar_prefetch=2,
            grid=(S // tm,),
            in_specs=[pl.BlockSpec((tm, D), lambda i, te, nt: (i, 0)),
                      wspec((1, D, F)), wspec((1, D, F)), wspec((1, F, D))],
            out_specs=pl.BlockSpec((tm, D), lambda i, te, nt: (i, 0)),
            scratch_shapes=[pltpu.VMEM((D, F), BF16), pltpu.VMEM((D, F), BF16), pltpu.VMEM((F, D), BF16)]),
        out_shape=jax.ShapeDtypeStruct((S, D), F32),
        compiler_params=_cparams("arbitrary"),
        name="moe_ffn",
    )(tile_e, n_tiles, xs, wg, wu, wd)


def _moe_out_kernel(slot_ref, slotn_ref, x1_ref, wcol_ref, g2_ref, lg_ref, lb_ref, ys_hbm, o_ref, ybuf, sem):
    t = pl.program_id(0)
    th = x1_ref.shape[1] // 2

    def issue(s_ref, half):
        for r in range(th):
            for k in range(2):
                pltpu.make_async_copy(ys_hbm.at[pl.ds(s_ref[0, k, half * th + r], 1)],
                                      ybuf.at[half, k, pl.ds(r, 1)], sem.at[half]).start(priority=k)

    def finish(half):
        for k in range(2):
            pltpu.make_async_copy(ys_hbm.at[pl.ds(0, th)], ybuf.at[half, k], sem.at[half]).wait()
        rows = pl.ds(half * th, th)
        w = wcol_ref[0, rows, :]
        moe = w[:, 0:1] * ybuf[half, 0] + w[:, 1:2] * ybuf[half, 1]
        o_ref[0, rows, :] = _ln(ALPHA * x1_ref[0, rows, :] + g2_ref[0] * moe) * lg_ref[...] + lb_ref[...]

    @pl.when(t == 0)
    def _():
        issue(slot_ref, 0)

    issue(slot_ref, 1)
    finish(0)

    @pl.when(t + 1 < pl.num_programs(0))
    def _():
        issue(slotn_ref, 0)

    finish(1)


def _moe_out_call(x1, ys, slots, wcol, mod, mod_row, ln_g, ln_b, tn):
    B, N, D = x1.shape
    nt = N // tn
    last = B * nt - 1
    row = (lambda t: t // nt) if mod_row is None else (lambda t: mod_row)
    sspec = lambda f: pl.BlockSpec((1, 2, tn), lambda t: (f(t), 0, 0), memory_space=pltpu.SMEM)
    vec = pl.BlockSpec((1, D), lambda t: (0, 0))
    return pl.pallas_call(
        _moe_out_kernel,
        grid=(B * nt,),
        in_specs=[sspec(lambda t: t), sspec(lambda t: jnp.minimum(t + 1, last)),
                  pl.BlockSpec((1, tn, D), lambda t: (t // nt, t % nt, 0)),
                  pl.BlockSpec((1, tn, 128), lambda t: (t // nt, t % nt, 0)),
                  pl.BlockSpec((1, 1, D), lambda t: (row(t), 0, 5)),
                  vec, vec,
                  pl.BlockSpec(memory_space=pl.ANY)],
        out_specs=pl.BlockSpec((1, tn, D), lambda t: (t // nt, t % nt, 0)),
        out_shape=jax.ShapeDtypeStruct((B, N, D), F32),
        scratch_shapes=[pltpu.VMEM((2, 2, tn // 2, D), F32), pltpu.SemaphoreType.DMA((2,))],
        compiler_params=_cparams("arbitrary"),
        name="moe_out",
    )(slots, slots, x1, wcol, mod, ln_g.reshape(1, D), ln_b.reshape(1, D), ys)


def _moe_sparse(x1, h2, wcol, er, cnt, mod, mod_row, weights, ln_g, ln_b, tn):
    B, N, D = x1.shape
    slots, tile_e, tiles = _slot_table(er, cnt, MOE_ROWS)
    xs = _dispatch_call(h2, _slot_blocks(slots, tn), tiles, 2 * B * N + N_EXPERTS * MOE_ROWS, MOE_ROWS, tn)
    ys = _gffn_call(xs, tile_e, tiles[2], weights, MOE_ROWS)
    tn_out = _pick_tile(N, 2 * tn)
    return _moe_out_call(x1, ys, _slot_blocks(slots, tn_out), wcol, mod, mod_row, ln_g, ln_b, tn_out)


def _inproj_weights(w_in_l):
    head, wk, wv = w_in_l[:, :U_WK], w_in_l[:, U_WK:U_WK + 2 * WIN_DH], w_in_l[:, U_WK + 2 * WIN_DH:]
    dup = lambda w: jnp.concatenate([w[:, :WIN_DH], w[:, :WIN_DH], w[:, WIN_DH:], w[:, WIN_DH:]], axis=1)
    return jnp.concatenate([head, dup(wk), dup(wv)], axis=1).astype(BF16)


def _rope_tables(N):
    rows = N // GRID_W
    row_pos = jnp.repeat(jnp.arange(rows, dtype=F32), GRID_W)
    col_pos = jnp.tile(jnp.arange(GRID_W, dtype=F32), rows)
    out = []
    for dh in (DIFF_DH, WIN_DH):
        n_axis = dh // 4
        inv = ROPE_BASE ** (-jnp.arange(n_axis, dtype=F32) / n_axis)
        ang = jnp.concatenate([row_pos[:, None] * inv, col_pos[:, None] * inv], -1)
        cos = jnp.repeat(jnp.cos(ang), 2, axis=-1)
        sin = jnp.repeat(jnp.sin(ang), 2, axis=-1) * jnp.tile(jnp.array([-1.0, 1.0], F32), dh // 2)
        out += [jnp.tile(cos, (1, 256 // dh)), jnp.tile(sin, (1, 256 // dh))]
    return tuple(out)


def _pick_tile(n, pref):
    return pref if n % pref == 0 else n


def kernel(x, c, ctx, c_ctx, w_mod, b_mod, w_in, w_out, conv_w, conv_b, conv_norm_g, conv_norm_b, diff_lambda,
           diff_subln_g, win_sink, ln_mix_g, ln_mix_b, ln_ffn_g, ln_ffn_b, router_w, router_bias, exp_w_gate,
           exp_w_up, exp_w_down):
    B, N, D = x.shape
    C = ctx.shape[1]
    ctx_row = B
    pad_rows = (-(B + 1)) % 8
    cs = jnp.concatenate([c, c_ctx[None, :], jnp.zeros((pad_rows, D), F32)], axis=0)
    mod_all = _mod_call(cs, w_mod, b_mod)
    tables = _rope_tables(N)
    dft = {n: _dft_tables(n) for n in {N, C}}
    router_wt = router_w.T
    xc = ctx
    for l in range(DEPTH):
        need_ctx = l < DEPTH - 1
        mod = mod_all[l].reshape(-1, 1, 6 * D)
        lam_init = 0.8 - 0.6 * math.exp(-0.3 * l)
        w_ext = _inproj_weights(w_in[l])
        w_o = w_out[l].astype(BF16)
        experts = (l, exp_w_gate, exp_w_up, exp_w_down)

        u = _inproj_call(x, mod, None, w_ext, tables, _pick_tile(N, 1024))
        uc = _inproj_call(xc, mod, ctx_row, w_ext, None, _pick_tile(C, 256))

        def mixers(u_lat):
            src = u_lat if u_lat is not None else uc
            n = src.shape[1]
            return (_conv_call(src, conv_w[l], conv_b[l], conv_norm_g[l], conv_norm_b[l]),
                    _diff_call(u_lat, uc, diff_lambda[l], diff_subln_g[l], lam_init, _pick_tile(n, 512)),
                    _fourier_call(src, dft[n], _pick_tile(n, 512)),
                    _win_call(u_lat, uc, win_sink[l], _pick_tile(n, 256)))

        routed = _outproj_call(mixers(u), w_o, x, mod, None, ln_mix_g[l], ln_mix_b[l], router_wt, router_bias,
                               _pick_tile(N, 1024))
        x = _moe_sparse(*routed, mod, None, experts, ln_ffn_g[l], ln_ffn_b[l], _pick_tile(N, 512))
        if need_ctx:
            routed = _outproj_call(mixers(None), w_o, xc, mod, ctx_row, ln_mix_g[l], ln_mix_b[l], router_wt,
                                   router_bias, _pick_tile(C, 256))
            xc = _moe_sparse(*routed, mod, ctx_row, experts, ln_ffn_g[l], ln_ffn_b[l], _pick_tile(C, 256))
    return x
```

```python
import functools
import math

import numpy as np
import jax
import jax.numpy as jnp
from jax import lax
from jax.experimental import pallas as pl
from jax.experimental.pallas import tpu as pltpu

F32 = jnp.float32
BF16 = jnp.bfloat16

D_MODEL = 1024
DEPTH = 2
GRID_W = 64
CONV_CH = 256
CONV_WIDTH = 31
DIFF_HEADS = 4
DIFF_DH = 32
FOURIER_GROUPS = 4
FOURIER_CH = 64
WIN_DH = 64
WIN_KV_HEADS = 2
WINDOW = 128
ROPE_BASE = 10000.0
N_EXPERTS = 16
N_EXPERT_GROUPS = 4
EXPERTS_PER_GROUP = 4
D_EXPERT = 512
ALPHA = (2 * DEPTH) ** 0.25
LN_EPS = 1e-5

U_AV, U_AG, U_DQ, U_DK, U_DV, U_FZ, U_WQ, U_WK, U_WV = (256 * i for i in range(9))
U_WIDTH = 9 * 256
ROPE_GROUPS = (U_DQ, U_DK, U_WQ, U_WK)
V7X_VMEM_LIMIT = 48 * 1024 * 1024


def _cparams(*sem):
    return pltpu.CompilerParams(dimension_semantics=sem, vmem_limit_bytes=V7X_VMEM_LIMIT)


def _ln(x):
    mu = jnp.mean(x, axis=-1, keepdims=True)
    xc = x - mu
    var = jnp.mean(xc * xc, axis=-1, keepdims=True)
    return xc * lax.rsqrt(var + LN_EPS)


def _sigmoid(x):
    return 1.0 / (1.0 + jnp.exp(-x))


def _dot(a, b):
    return jnp.dot(a, b, preferred_element_type=F32)


def _dot_nt(a, b):
    return lax.dot_general(a, b, (((1,), (1,)), ((), ())), preferred_element_type=F32)


def _mod_kernel(c_ref, w_ref, b_ref, o_ref):
    c = c_ref[...]
    s = c * _sigmoid(c)
    o_ref[0] = jnp.dot(s, w_ref[0], preferred_element_type=F32, precision=lax.Precision.HIGHEST) + b_ref[0]


def _mod_call(cs, w_mod, b_mod):
    R, D = cs.shape
    return pl.pallas_call(
        _mod_kernel,
        grid=(DEPTH, 6),
        in_specs=[pl.BlockSpec((R, D), lambda l, j: (0, 0)),
                  pl.BlockSpec((1, D, D), lambda l, j: (l, 0, j)),
                  pl.BlockSpec((1, 1, D), lambda l, j: (l, 0, j))],
        out_specs=pl.BlockSpec((1, R, D), lambda l, j: (l, 0, j)),
        out_shape=jax.ShapeDtypeStruct((DEPTH, R, 6 * D), F32),
        compiler_params=_cparams("arbitrary", "arbitrary"),
        name="mod",
    )(cs, w_mod, b_mod.reshape(DEPTH, 1, 6 * D))


def _inproj_kernel(*refs, rope):
    if rope:
        x_ref, sh_ref, sc_ref, w_ref, cd_ref, sd_ref, cw_ref, sw_ref, o_ref = refs
    else:
        x_ref, sh_ref, sc_ref, w_ref, o_ref = refs
    h = (_ln(x_ref[0]) * (1.0 + sc_ref[0]) + sh_ref[0]).astype(BF16)
    q_scale = {U_DQ: DIFF_DH ** -0.5 * math.log2(math.e), U_WQ: WIN_DH ** -0.5}
    even = lax.broadcasted_iota(jnp.int32, (1, 128), 1) % 2 == 0
    for a in range(0, U_WIDTH, 256):
        val = _dot(h, w_ref[:, a:a + 256])
        if a in q_scale:
            val = val * q_scale[a]
        if rope and a in ROPE_GROUPS:
            cos, sin = (cd_ref, sd_ref) if a in (U_DQ, U_DK) else (cw_ref, sw_ref)
            for half in range(2):
                v = val[:, 128 * half:128 * (half + 1)]
                partner = jnp.where(even, pltpu.roll(v, 127, axis=1), pltpu.roll(v, 1, axis=1))
                o_ref[0, :, a + 128 * half:a + 128 * (half + 1)] = (
                    v * cos[:, 128 * half:128 * (half + 1)] + partner * sin[:, 128 * half:128 * (half + 1)]).astype(BF16)
        else:
            o_ref[0, :, a:a + 256] = val.astype(BF16)


def _inproj_call(x, mod, mod_row, w, tables, tn):
    B, N, D = x.shape
    rope = tables is not None
    row = (lambda b: b) if mod_row is None else (lambda b: mod_row)
    in_specs = [pl.BlockSpec((1, tn, D), lambda b, i: (b, i, 0)),
                pl.BlockSpec((1, 1, D), lambda b, i: (row(b), 0, 0)),
                pl.BlockSpec((1, 1, D), lambda b, i: (row(b), 0, 1)),
                pl.BlockSpec(w.shape, lambda b, i: (0, 0))]
    args = [x, mod, mod, w]
    if rope:
        in_specs += [pl.BlockSpec((tn, 256), lambda b, i: (i, 0))] * 4
        args += list(tables)
    return pl.pallas_call(
        functools.partial(_inproj_kernel, rope=rope),
        grid=(B, N // tn),
        in_specs=in_specs,
        out_specs=pl.BlockSpec((1, tn, U_WIDTH), lambda b, i: (b, i, 0)),
        out_shape=jax.ShapeDtypeStruct((B, N, U_WIDTH), BF16),
        compiler_params=_cparams("parallel", "parallel"),
        name="inproj_rope" if rope else "inproj",
    )(*args)


CONV_ROWS = 128
CONV_PAD = 16


def _conv_kernel(u_ref, w_ref, b_ref, g_ref, nb_ref, o_ref, pad_ref, sh_ref):
    N = u_ref.shape[1]
    L = N + 2 * CONV_PAD
    val = u_ref[0, :, 0:CONV_CH].astype(F32)
    gate = u_ref[0, :, CONV_CH:2 * CONV_CH].astype(F32)
    pad_ref[0:CONV_PAD, :] = jnp.zeros((CONV_PAD, CONV_CH), F32)
    pad_ref[CONV_PAD + N:L + 8, :] = jnp.zeros((CONV_PAD + 8, CONV_CH), F32)
    pad_ref[CONV_PAD:CONV_PAD + N, :] = val * _sigmoid(gate)
    for s in range(8):
        sh_ref[s] = pad_ref[s:s + L, :]

    def tile(i, carry):
        r0 = pl.multiple_of(i * CONV_ROWS, CONV_ROWS)
        acc = jnp.zeros((CONV_ROWS, CONV_CH), F32)
        for k in range(CONV_WIDTH):
            off = CONV_PAD - CONV_WIDTH // 2 + k
            acc = acc + w_ref[k:k + 1, :] * sh_ref[off % 8, pl.ds(r0 + 8 * (off // 8), CONV_ROWS), :]
        y = _ln(acc + b_ref[...]) * g_ref[...] + nb_ref[...]
        o_ref[0, pl.ds(r0, CONV_ROWS), :] = (y * _sigmoid(y)).astype(BF16)
        return carry

    lax.fori_loop(0, N // CONV_ROWS, tile, 0)


def _conv_call(u, conv_w, conv_b, conv_ng, conv_nb):
    B, N, _ = u.shape
    vec = pl.BlockSpec((1, CONV_CH), lambda b: (0, 0))
    return pl.pallas_call(
        _conv_kernel,
        grid=(B,),
        in_specs=[pl.BlockSpec((1, N, 2 * CONV_CH), lambda b: (b, 0, 0)),
                  pl.BlockSpec((CONV_WIDTH, CONV_CH), lambda b: (0, 0)), vec, vec, vec],
        out_specs=pl.BlockSpec((1, N, CONV_CH), lambda b: (b, 0, 0)),
        out_shape=jax.ShapeDtypeStruct((B, N, CONV_CH), BF16),
        scratch_shapes=[pltpu.VMEM((N + 2 * CONV_PAD + 8, CONV_CH), F32),
                        pltpu.VMEM((8, N + 2 * CONV_PAD, CONV_CH), F32)],
        compiler_params=_cparams("parallel"),
        name="conv",
    )(u, conv_w, conv_b.reshape(1, -1), conv_ng.reshape(1, -1), conv_nb.reshape(1, -1))


def _diff_kernel(*refs, lam_init, has_lat):
    if has_lat:
        q_ref, k_ref, v_ref, kc_ref, vc_ref, dl_ref, g_ref, o_ref, kall_ref, vt_ref = refs
    else:
        q_ref, kc_ref, vc_ref, dl_ref, g_ref, o_ref, kall_ref, vt_ref = refs
    C = kc_ref.shape[1]
    DV = 2 * DIFF_DH

    @pl.when(pl.program_id(1) == 0)
    def _():
        kall_ref[0:C, :] = kc_ref[0]
        vt_ref[:, 0:C] = vc_ref[0].astype(F32).T.astype(BF16)
        if has_lat:
            N = k_ref.shape[1]
            kall_ref[C:C + N, :] = k_ref[0]
            vt_ref[:, C:C + N] = v_ref[0].astype(F32).T.astype(BF16)

    q = q_ref[0]
    kall = kall_ref[...]
    dl = dl_ref[...]
    lam = (jnp.exp(jnp.sum(dl[0:1] * dl[1:2], axis=-1, keepdims=True))
           - jnp.exp(jnp.sum(dl[2:3] * dl[3:4], axis=-1, keepdims=True)) + lam_init)
    lane = lax.broadcasted_iota(jnp.int32, (1, DIFF_HEADS * DV), 1)
    scores = []
    for hc in range(2 * DIFF_HEADS):
        lo = hc * DIFF_DH
        km = jnp.where((lane >= lo) & (lane < lo + DIFF_DH), kall, jnp.zeros_like(kall))
        scores.append(_dot_nt(km, q))
    parts = []
    for st in scores:
        e = jnp.exp2(st - jnp.max(st, axis=0, keepdims=True))
        parts.append((e, jnp.sum(e, axis=0, keepdims=True)))
    outs = []
    for h in range(DIFF_HEADS):
        (e0, l0), (e1, l1) = parts[2 * h], parts[2 * h + 1]
        at = (e0 - (lam * l0 / l1) * e1).astype(BF16)
        ot = _dot(vt_ref[DV * h:DV * (h + 1), :], at) * (1.0 / l0)
        ms = jnp.mean(ot * ot, axis=0, keepdims=True)
        outs.append(ot * lax.rsqrt(ms + LN_EPS))
    yt = jnp.concatenate(outs, axis=0) * g_ref[...] * (1.0 - lam_init)
    o_ref[0] = yt.T.astype(BF16)


def _diff_call(u, uc, diff_lambda, subln_g, lam_init, tq):
    has_lat = u is not None
    src = u if has_lat else uc
    B, N, _ = src.shape
    C = uc.shape[1]
    W = DIFF_HEADS * 2 * DIFF_DH
    n_keys = C + N if has_lat else C
    blk = lambda col: U_DQ // 256 + col
    in_specs = [pl.BlockSpec((1, tq, W), lambda b, i: (b, i, blk(0)))]
    args = [src]
    if has_lat:
        in_specs += [pl.BlockSpec((1, N, W), lambda b, i: (b, 0, blk(1))),
                     pl.BlockSpec((1, N, W), lambda b, i: (b, 0, blk(2)))]
        args += [u, u]
    in_specs += [pl.BlockSpec((1, C, W), lambda b, i: (b, 0, blk(1))),
                 pl.BlockSpec((1, C, W), lambda b, i: (b, 0, blk(2))),
                 pl.BlockSpec((4, DIFF_DH), lambda b, i: (0, 0)),
                 pl.BlockSpec((W, 1), lambda b, i: (0, 0))]
    args += [uc, uc, diff_lambda, jnp.tile(subln_g, DIFF_HEADS).reshape(W, 1)]
    return pl.pallas_call(
        functools.partial(_diff_kernel, lam_init=lam_init, has_lat=has_lat),
        grid=(B, N // tq),
        in_specs=in_specs,
        out_specs=pl.BlockSpec((1, tq, W), lambda b, i: (b, i, 0)),
        out_shape=jax.ShapeDtypeStruct((B, N, W), BF16),
        scratch_shapes=[pltpu.VMEM((n_keys, W), BF16), pltpu.VMEM((W, n_keys), BF16)],
        compiler_params=_cparams("parallel", "arbitrary"),
        name="diff_attn" if has_lat else "diff_attn_ctx",
    )(*args)


def _fourier_kernel(z_ref, cs_ref, cn_ref, sn_ref, o_ref, a_ref, b_ref, *, scale):
    W = FOURIER_GROUPS * FOURIER_CH

    @pl.when(pl.program_id(1) == 0)
    def _():
        t = _dot(z_ref[0], cs_ref[...])
        a_ref[...] = t[:, :W].astype(BF16)
        b_ref[...] = t[:, W:].astype(BF16)

    y = _dot(cn_ref[...], a_ref[...]) - _dot(sn_ref[...], b_ref[...])
    o_ref[0] = (y * scale).astype(BF16)


def _dft_tables(N):
    n = jnp.arange(N, dtype=jnp.int32)
    ang = ((n[:, None] * n[None, :]) % N).astype(F32) * (2.0 * math.pi / N)
    c = np.arange(FOURIER_CH, dtype=np.int64)
    angc = 2.0 * np.pi * ((c[:, None] * c[None, :]) % FOURIER_CH).astype(np.float64) / FOURIER_CH
    eye = np.eye(FOURIER_GROUPS)
    cs = np.concatenate([np.kron(eye, np.cos(angc)), np.kron(eye, np.sin(angc))], axis=1)
    return jnp.asarray(cs, F32).astype(BF16), jnp.cos(ang).astype(BF16), jnp.sin(ang).astype(BF16)


def _fourier_call(u, dft, tm):
    B, N, _ = u.shape
    W = FOURIER_GROUPS * FOURIER_CH
    cs, cn, sn = dft
    return pl.pallas_call(
        functools.partial(_fourier_kernel, scale=float((N * FOURIER_CH) ** -0.5)),
        grid=(B, N // tm),
        in_specs=[pl.BlockSpec((1, N, W), lambda b, i: (b, 0, U_FZ // 256)),
                  pl.BlockSpec((W, 2 * W), lambda b, i: (0, 0)),
                  pl.BlockSpec((tm, N), lambda b, i: (i, 0)),
                  pl.BlockSpec((tm, N), lambda b, i: (i, 0))],
        out_specs=pl.BlockSpec((1, tm, W), lambda b, i: (b, i, 0)),
        out_shape=jax.ShapeDtypeStruct((B, N, W), BF16),
        scratch_shapes=[pltpu.VMEM((N, W), BF16), pltpu.VMEM((N, W), BF16)],
        compiler_params=_cparams("parallel", "arbitrary"),
        name="fourier",
    )(u, cs, cn, sn)


def _win_kernel(*refs, tq, has_local):
    if has_local:
        sink_ref, q_ref, k_ref, v_ref, kc_ref, vc_ref, o_ref = refs
    else:
        sink_ref, q_ref, kc_ref, vc_ref, o_ref = refs
    g, i = pl.program_id(1), pl.program_id(2)
    q = q_ref[0]
    lane = lax.broadcasted_iota(jnp.int32, (1, 2 * WIN_DH), 1)
    lo = lane < WIN_DH
    zero = jnp.zeros_like(q)
    q2 = jnp.concatenate([jnp.where(lo, q, zero), jnp.where(lo, zero, q)], axis=0)
    row = lax.broadcasted_iota(jnp.int32, (2 * tq, 1), 0)
    sink = jnp.where(row < tq, sink_ref[2 * g], sink_ref[2 * g + 1])
    kc, vc = kc_ref[0], vc_ref[0]
    s_c = _dot_nt(q2, kc)
    m = jnp.maximum(jnp.max(s_c, axis=-1, keepdims=True), sink)
    if has_local:
        N = k_ref.shape[1]
        span = tq + 2 * WINDOW
        start = pl.multiple_of(jnp.clip(i * tq - WINDOW, 0, N - span), WINDOW)
        kw = k_ref[0, pl.ds(start, span), :]
        vw = v_ref[0, pl.ds(start, span), :]
        s_l = _dot_nt(q2, kw)
        qpos = i * tq + jnp.where(row < tq, row, row - tq)
        kpos = start + lax.broadcasted_iota(jnp.int32, (1, span), 1)
        s_l = jnp.where(jnp.abs(kpos - qpos) <= WINDOW, s_l, -jnp.inf)
        m = jnp.maximum(m, jnp.max(s_l, axis=-1, keepdims=True))
    e_c = jnp.exp(s_c - m)
    den = jnp.sum(e_c, axis=-1, keepdims=True) + jnp.exp(sink - m)
    if has_local:
        e_l = jnp.exp(s_l - m)
        den = den + jnp.sum(e_l, axis=-1, keepdims=True)
    inv = 1.0 / den
    o2 = _dot((e_c * inv).astype(BF16), vc)
    if has_local:
        o2 = o2 + _dot((e_l * inv).astype(BF16), vw)
    o_ref[0] = jnp.where(lo, o2[:tq], o2[tq:]).astype(BF16)


def _win_call(u, uc, sink, tq):
    has_local = u is not None
    src = u if has_local else uc
    B, N, _ = src.shape
    C = uc.shape[1]
    W = 2 * WIN_DH
    G = WIN_KV_HEADS
    in_specs = [pl.BlockSpec(memory_space=pltpu.SMEM),
                pl.BlockSpec((1, tq, W), lambda b, g, i: (b, i, U_WQ // W + g))]
    args = [sink, src]
    if has_local:
        in_specs += [pl.BlockSpec((1, N, W), lambda b, g, i: (b, 0, U_WK // W + g)),
                     pl.BlockSpec((1, N, W), lambda b, g, i: (b, 0, U_WV // W + g))]
        args += [u, u]
    in_specs += [pl.BlockSpec((1, C, W), lambda b, g, i: (b, 0, U_WK // W + g)),
                 pl.BlockSpec((1, C, W), lambda b, g, i: (b, 0, U_WV // W + g))]
    args += [uc, uc]
    return pl.pallas_call(
        functools.partial(_win_kernel, tq=tq, has_local=has_local),
        grid=(B, G, N // tq),
        in_specs=in_specs,
        out_specs=pl.BlockSpec((1, tq, W), lambda b, g, i: (b, i, g)),
        out_shape=jax.ShapeDtypeStruct((B, N, G * W), BF16),
        compiler_params=_cparams("parallel", "parallel", "parallel"),
        name="win_attn" if has_local else "win_attn_ctx",
    )(*args)


def _route(sel, score):
    EG = EXPERTS_PER_GROUP
    rows = [sel[e:e + 1] for e in range(N_EXPERTS)]
    srow = [score[e:e + 1] for e in range(N_EXPERTS)]
    best, best_score = None, None
    for g in range(N_EXPERT_GROUPS):
        r = rows[EG * g:EG * (g + 1)]
        top2 = None
        for a in range(EG):
            for b in range(a + 1, EG):
                pair = r[a] + r[b]
                top2 = pair if top2 is None else jnp.maximum(top2, pair)
        if g == 0:
            best, best_score = jnp.zeros(top2.shape, jnp.int32), top2
        else:
            upd = top2 > best_score
            best = jnp.where(upd, g, best)
            best_score = jnp.where(upd, top2, best_score)

    def pick(vals, j):
        out = vals[j]
        for g in range(1, N_EXPERT_GROUPS):
            out = jnp.where(best == g, vals[EG * g + j], out)
        return out

    neg = jnp.full(best_score.shape, -jnp.inf, F32)
    m1, m2 = neg, neg
    i1 = i2 = jnp.zeros(best.shape, jnp.int32)
    w1 = w2 = jnp.zeros(best_score.shape, F32)
    for j in range(EG):
        v, sc = pick(rows, j), pick(srow, j)
        gt1 = v > m1
        gt2 = v > m2
        m2 = jnp.where(gt1, m1, jnp.where(gt2, v, m2))
        i2 = jnp.where(gt1, i1, jnp.where(gt2, j, i2))
        w2 = jnp.where(gt1, w1, jnp.where(gt2, sc, w2))
        m1 = jnp.where(gt1, v, m1)
        i1 = jnp.where(gt1, j, i1)
        w1 = jnp.where(gt1, sc, w1)
    tot = w1 + w2
    return best * EG + i1, best * EG + i2, w1 / tot, w2 / tot


def _outproj_kernel(ya_ref, yb_ref, yc_ref, yd_ref, w_ref, x_ref, g1_ref, sh2_ref, sc2_ref, lg_ref, lb_ref,
                    rw_ref, rb_ref, x1_ref, h2_ref, wcol_ref, e_ref, cnt_ref, seen_ref):
    tn = x_ref.shape[1]
    i = pl.program_id(1)

    @pl.when(i == 0)
    def _():
        seen_ref[...] = jnp.zeros_like(seen_ref)

    y = _dot(ya_ref[0], w_ref[0:256, :])
    y = y + _dot(yb_ref[0], w_ref[256:512, :])
    y = y + _dot(yc_ref[0], w_ref[512:768, :])
    y = y + _dot(yd_ref[0], w_ref[768:1024, :])
    x1 = _ln(ALPHA * x_ref[0] + g1_ref[0] * y) * lg_ref[...] + lb_ref[...]
    x1_ref[0] = x1
    h2 = _ln(x1) * (1.0 + sc2_ref[0]) + sh2_ref[0]
    logits = lax.dot_general(rw_ref[...], h2, (((1,), (1,)), ((), ())), preferred_element_type=F32,
                             precision=lax.Precision.HIGHEST)
    h2_ref[0] = h2
    score = _sigmoid(logits)
    e1, e2, w1, w2 = _route(score + rb_ref[...], score)
    wcol_ref[0] = jnp.concatenate([w1, w2, jnp.zeros((126, tn), F32)], axis=0).T
    erow = lax.broadcasted_iota(jnp.int32, (N_EXPERTS, tn), 0)
    hit1, hit2 = erow == e1, erow == e2
    chosen = jnp.where(hit1, 1.0, 0.0) + jnp.where(hit2, 1.0, 0.0)
    earlier = jnp.where(lax.broadcasted_iota(jnp.int32, (tn, tn), 0) < lax.broadcasted_iota(jnp.int32, (tn, tn), 1),
                        1.0, 0.0).astype(BF16)
    before = _dot(chosen.astype(BF16), earlier) + seen_ref[...]
    r1 = jnp.sum(jnp.where(hit1, before, 0.0), axis=0, keepdims=True)
    r2 = jnp.sum(jnp.where(hit2, before, 0.0), axis=0, keepdims=True)
    e_ref[0] = jnp.concatenate([e1, e2, r1.astype(jnp.int32), r2.astype(jnp.int32), jnp.zeros((4, tn), jnp.int32)],
                               axis=0)
    seen_ref[...] += jnp.sum(chosen, axis=1, keepdims=True)

    @pl.when(i == pl.num_programs(1) - 1)
    def _():
        cnt_ref[0] = jnp.broadcast_to(seen_ref[...], (N_EXPERTS, 128))


def _outproj_call(ys, w_out, x, mod, mod_row, ln_g, ln_b, router_wt, router_bias, tn):
    B, N, D = x.shape
    row = (lambda b: b) if mod_row is None else (lambda b: mod_row)
    yspec = pl.BlockSpec((1, tn, 256), lambda b, i: (b, i, 0))
    mspec = lambda j: pl.BlockSpec((1, 1, D), lambda b, i: (row(b), 0, j))
    vec = pl.BlockSpec((1, D), lambda b, i: (0, 0))
    return pl.pallas_call(
        _outproj_kernel,
        grid=(B, N // tn),
        in_specs=[yspec, yspec, yspec, yspec,
                  pl.BlockSpec((D, D), lambda b, i: (0, 0)),
                  pl.BlockSpec((1, tn, D), lambda b, i: (b, i, 0)),
                  mspec(2), mspec(3), mspec(4), vec, vec,
                  pl.BlockSpec((N_EXPERTS, D), lambda b, i: (0, 0)),
                  pl.BlockSpec((N_EXPERTS, 1), lambda b, i: (0, 0))],
        out_specs=[pl.BlockSpec((1, tn, D), lambda b, i: (b, i, 0)),
                   pl.BlockSpec((1, tn, D), lambda b, i: (b, i, 0)),
                   pl.BlockSpec((1, tn, 128), lambda b, i: (b, i, 0)),
                   pl.BlockSpec((1, 8, tn), lambda b, i: (b, 0, i)),
                   pl.BlockSpec((1, N_EXPERTS, 128), lambda b, i: (b, 0, 0))],
        out_shape=[jax.ShapeDtypeStruct((B, N, D), F32), jax.ShapeDtypeStruct((B, N, D), F32),
                   jax.ShapeDtypeStruct((B, N, 128), F32), jax.ShapeDtypeStruct((B, 8, N), jnp.int32),
                   jax.ShapeDtypeStruct((B, N_EXPERTS, 128), F32)],
        scratch_shapes=[pltpu.VMEM((N_EXPERTS, 1), F32)],
        compiler_params=_cparams("parallel", "arbitrary"),
        name="outproj_router",
    )(*ys, w_out, x, mod, mod, mod, ln_g.reshape(1, D), ln_b.reshape(1, D), router_wt, router_bias.reshape(-1, 1))


MOE_ROWS = 512


def _slot_blocks(slots, tn):
    B, _, N = slots.shape
    return slots.reshape(B, 2, N // tn, tn).transpose(0, 2, 1, 3).reshape(-1, 2, tn)


def _slot_table(er, cnt, tm):
    B, _, N = er.shape
    E = N_EXPERTS
    c = cnt[:, :, 0].astype(jnp.int32)
    counts = jnp.sum(c, axis=0)
    ptiles = (counts + tm - 1) // tm
    pend = jnp.cumsum(ptiles) * tm
    base = (pend - ptiles * tm)[None, :] + jnp.cumsum(c, axis=0) - c
    eids = jnp.arange(E, dtype=jnp.int32)[None, :, None]
    slots = [jnp.sum(jnp.where(er[:, k, None, :] == eids, base[:, :, None], 0), axis=1) + er[:, 2 + k, :]
             for k in range(2)]
    slots = jnp.stack(slots, axis=1)
    tile0 = jnp.arange((2 * B * N + E * tm) // tm, dtype=jnp.int32) * tm
    tile_e = jnp.minimum(jnp.sum((pend[None, :] <= tile0[:, None]).astype(jnp.int32), axis=1), E - 1)
    n_tiles = jnp.sum(ptiles).astype(jnp.int32).reshape(1)
    return slots.astype(jnp.int32), tile_e, (pend.astype(jnp.int32), ptiles.astype(jnp.int32), n_tiles)


def _dispatch_kernel(pend_ref, ptiles_ref, nt_ref, slot_ref, h_ref, xs_hbm, zbuf, sem, zsem):
    tn = h_ref.shape[1]
    tm = zbuf.shape[0]
    n_max = xs_hbm.shape[0] // tm

    @pl.when((pl.program_id(0) == 0) & (pl.program_id(1) == 0))
    def _():
        zbuf[...] = jnp.zeros_like(zbuf)

        def fill(row0):
            return pltpu.make_async_copy(zbuf, xs_hbm.at[pl.ds(pl.multiple_of(row0, tm), tm)], zsem.at[0])

        def each_fill(act):
            for e in range(N_EXPERTS):
                @pl.when(ptiles_ref[e] > 0)
                def _():
                    act(fill(pend_ref[e] - tm))

                @pl.when(nt_ref[0] + e < n_max)
                def _():
                    act(fill((nt_ref[0] + e) * tm))

        each_fill(lambda copy: copy.start())
        each_fill(lambda copy: copy.wait())

    for r in range(tn):
        for k in range(2):
            pltpu.make_async_copy(h_ref.at[0, pl.ds(r, 1)], xs_hbm.at[pl.ds(slot_ref[0, k, r], 1)],
                                  sem.at[0]).start(priority=k)
    for k in range(2):
        pltpu.make_async_copy(h_ref.at[0], xs_hbm.at[pl.ds(0, tn)], sem.at[0]).wait()


def _dispatch_call(h2, slots, tiles, n_rows, tm, tn):
    B, N, D = h2.shape
    nt = N // tn
    return pl.pallas_call(
        _dispatch_kernel,
        grid_spec=pltpu.PrefetchScalarGridSpec(
            num_scalar_prefetch=3,
            grid=(B, nt),
            in_specs=[pl.BlockSpec((1, 2, tn), lambda b, i, *_: (b * nt + i, 0, 0), memory_space=pltpu.SMEM),
                      pl.BlockSpec((1, tn, D), lambda b, i, *_: (b, i, 0))],
            out_specs=pl.BlockSpec(memory_space=pl.ANY),
            scratch_shapes=[pltpu.VMEM((tm, D), F32), pltpu.SemaphoreType.DMA((1,)), pltpu.SemaphoreType.DMA((1,))]),
        out_shape=jax.ShapeDtypeStruct((n_rows, D), F32),
        compiler_params=_cparams("arbitrary", "arbitrary"),
        name="moe_dispatch",
    )(*tiles, slots, h2)


def _gffn_kernel(te_ref, nt_ref, x_ref, wg_ref, wu_ref, wd_ref, o_ref, wg_bf, wu_bf, wd_bf):
    i = pl.program_id(0)

    @pl.when((i == 0) | (te_ref[i] != te_ref[jnp.maximum(i - 1, 0)]))
    def _():
        wg_bf[...] = wg_ref[0].astype(BF16)
        wu_bf[...] = wu_ref[0].astype(BF16)
        wd_bf[...] = wd_ref[0].astype(BF16)

    @pl.when(i < nt_ref[0])
    def _():
        x = x_ref[...].astype(BF16)
        gate = _dot(x, wg_bf[...])
        up = _dot(x, wu_bf[...])
        act = (gate * _sigmoid(gate) * up).astype(BF16)
        o_ref[...] = _dot(act, wd_bf[...])

    @pl.when(i >= nt_ref[0])
    def _():
        o_ref[...] = jnp.zeros_like(o_ref)


def _gffn_call(xs, tile_e, n_tiles, weights, tm):
    S, D = xs.shape
    layer, wg, wu, wd = weights
    F = wg.shape[-1]
    wspec = lambda shape: pl.BlockSpec((None,) + shape, lambda i, te, nt: (layer, te[i], 0, 0))
    return pl.pallas_call(
        _gffn_kernel,
        grid_spec=pltpu.PrefetchScalarGridSpec(
            num_scalar_prefetch=2,
            grid=(S // tm,),
            in_specs=[pl.BlockSpec((tm, D), lambda i, te, nt: (i, 0)),
                      wspec((1, D, F)), wspec((1, D, F)), wspec((1, F, D))],
            out_specs=pl.BlockSpec((tm, D), lambda i, te, nt: (i, 0)),
            scratch_shapes=[pltpu.VMEM((D, F), BF16), pltpu.VMEM((D, F), BF16), pltpu.VMEM((F, D), BF16)]),
        out_shape=jax.ShapeDtypeStruct((S, D), F32),
        compiler_params=_cparams("arbitrary"),
        name="moe_ffn",
    )(tile_e, n_tiles, xs, wg, wu, wd)


def _moe_out_kernel(slot_ref, slotn_ref, x1_ref, wcol_ref, g2_ref, lg_ref, lb_ref, ys_hbm, o_ref, ybuf, sem):
    t = pl.program_id(0)
    th = x1_ref.shape[1] // 2

    def issue(s_ref, half):
        for r in range(th):
            for k in range(2):
                pltpu.make_async_copy(ys_hbm.at[pl.ds(s_ref[0, k, half * th + r], 1)],
                                      ybuf.at[half, k, pl.ds(r, 1)], sem.at[half]).start(priority=k)

    def finish(half):
        for k in range(2):
            pltpu.make_async_copy(ys_hbm.at[pl.ds(0, th)], ybuf.at[half, k], sem.at[half]).wait()
        rows = pl.ds(half * th, th)
        w = wcol_ref[0, rows, :]
        moe = w[:, 0:1] * ybuf[half, 0] + w[:, 1:2] * ybuf[half, 1]
        o_ref[0, rows, :] = _ln(ALPHA * x1_ref[0, rows, :] + g2_ref[0] * moe) * lg_ref[...] + lb_ref[...]

    @pl.when(t == 0)
    def _():
        issue(slot_ref, 0)

    issue(slot_ref, 1)
    finish(0)

    @pl.when(t + 1 < pl.num_programs(0))
    def _():
        issue(slotn_ref, 0)

    finish(1)


def _moe_out_call(x1, ys, slots, wcol, mod, mod_row, ln_g, ln_b, tn):
    B, N, D = x1.shape
    nt = N // tn
    last = B * nt - 1
    row = (lambda t: t // nt) if mod_row is None else (lambda t: mod_row)
    sspec = lambda f: pl.BlockSpec((1, 2, tn), lambda t: (f(t), 0, 0), memory_space=pltpu.SMEM)
    vec = pl.BlockSpec((1, D), lambda t: (0, 0))
    return pl.pallas_call(
        _moe_out_kernel,
        grid=(B * nt,),
        in_specs=[sspec(lambda t: t), sspec(lambda t: jnp.minimum(t + 1, last)),
                  pl.BlockSpec((1, tn, D), lambda t: (t // nt, t % nt, 0)),
                  pl.BlockSpec((1, tn, 128), lambda t: (t // nt, t % nt, 0)),
                  pl.BlockSpec((1, 1, D), lambda t: (row(t), 0, 5)),
                  vec, vec,
                  pl.BlockSpec(memory_space=pl.ANY)],
        out_specs=pl.BlockSpec((1, tn, D), lambda t: (t // nt, t % nt, 0)),
        out_shape=jax.ShapeDtypeStruct((B, N, D), F32),
        scratch_shapes=[pltpu.VMEM((2, 2, tn // 2, D), F32), pltpu.SemaphoreType.DMA((2,))],
        compiler_params=_cparams("arbitrary"),
        name="moe_out",
    )(slots, slots, x1, wcol, mod, ln_g.reshape(1, D), ln_b.reshape(1, D), ys)


def _moe_sparse(x1, h2, wcol, er, cnt, mod, mod_row, weights, ln_g, ln_b, tn):
    B, N, D = x1.shape
    slots, tile_e, tiles = _slot_table(er, cnt, MOE_ROWS)
    xs = _dispatch_call(h2, _slot_blocks(slots, tn), tiles, 2 * B * N + N_EXPERTS * MOE_ROWS, MOE_ROWS, tn)
    ys = _gffn_call(xs, tile_e, tiles[2], weights, MOE_ROWS)
    tn_out = _pick_tile(N, 2 * tn)
    return _moe_out_call(x1, ys, _slot_blocks(slots, tn_out), wcol, mod, mod_row, ln_g, ln_b, tn_out)


def _inproj_weights(w_in_l):
    head, wk, wv = w_in_l[:, :U_WK], w_in_l[:, U_WK:U_WK + 2 * WIN_DH], w_in_l[:, U_WK + 2 * WIN_DH:]
    dup = lambda w: jnp.concatenate([w[:, :WIN_DH], w[:, :WIN_DH], w[:, WIN_DH:], w[:, WIN_DH:]], axis=1)
    return jnp.concatenate([head, dup(wk), dup(wv)], axis=1).astype(BF16)


def _rope_tables(N):
    rows = N // GRID_W
    row_pos = jnp.repeat(jnp.arange(rows, dtype=F32), GRID_W)
    col_pos = jnp.tile(jnp.arange(GRID_W, dtype=F32), rows)
    out = []
    for dh in (DIFF_DH, WIN_DH):
        n_axis = dh // 4
        inv = ROPE_BASE ** (-jnp.arange(n_axis, dtype=F32) / n_axis)
        ang = jnp.concatenate([row_pos[:, None] * inv, col_pos[:, None] * inv], -1)
        cos = jnp.repeat(jnp.cos(ang), 2, axis=-1)
        sin = jnp.repeat(jnp.sin(ang), 2, axis=-1) * jnp.tile(jnp.array([-1.0, 1.0], F32), dh // 2)
        out += [jnp.tile(cos, (1, 256 // dh)), jnp.tile(sin, (1, 256 // dh))]
    return tuple(out)


def _pick_tile(n, pref):
    return pref if n % pref == 0 else n


def kernel(x, c, ctx, c_ctx, w_mod, b_mod, w_in, w_out, conv_w, conv_b, conv_norm_g, conv_norm_b, diff_lambda,
           diff_subln_g, win_sink, ln_mix_g, ln_mix_b, ln_ffn_g, ln_ffn_b, router_w, router_bias, exp_w_gate,
           exp_w_up, exp_w_down):
    B, N, D = x.shape
    C = ctx.shape[1]
    ctx_row = B
    pad_rows = (-(B + 1)) % 8
    cs = jnp.concatenate([c, c_ctx[None, :], jnp.zeros((pad_rows, D), F32)], axis=0)
    mod_all = _mod_call(cs, w_mod, b_mod)
    tables = _rope_tables(N)
    dft = {n: _dft_tables(n) for n in {N, C}}
    router_wt = router_w.T
    xc = ctx
    for l in range(DEPTH):
        need_ctx = l < DEPTH - 1
        mod = mod_all[l].reshape(-1, 1, 6 * D)
        lam_init = 0.8 - 0.6 * math.exp(-0.3 * l)
        w_ext = _inproj_weights(w_in[l])
        w_o = w_out[l].astype(BF16)
        experts = (l, exp_w_gate, exp_w_up, exp_w_down)

        u = _inproj_call(x, mod, None, w_ext, tables, _pick_tile(N, 1024))
        uc = _inproj_call(xc, mod, ctx_row, w_ext, None, _pick_tile(C, 256))

        def mixers(u_lat):
            src = u_lat if u_lat is not None else uc
            n = src.shape[1]
            return (_conv_call(src, conv_w[l], conv_b[l], conv_norm_g[l], conv_norm_b[l]),
                    _diff_call(u_lat, uc, diff_lambda[l], diff_subln_g[l], lam_init, _pick_tile(n, 512)),
                    _fourier_call(src, dft[n], _pick_tile(n, 512)),
                    _win_call(u_lat, uc, win_sink[l], _pick_tile(n, 256)))

        routed = _outproj_call(mixers(u), w_o, x, mod, None, ln_mix_g[l], ln_mix_b[l], router_wt, router_bias,
                               _pick_tile(N, 1024))
        x = _moe_sparse(*routed, mod, None, experts, ln_ffn_g[l], ln_ffn_b[l], _pick_tile(N, 512))
        if need_ctx:
            routed = _outproj_call(mixers(None), w_o, xc, mod, ctx_row, ln_mix_g[l], ln_mix_b[l], router_wt,
                                   router_bias, _pick_tile(C, 256))
            xc = _moe_sparse(*routed, mod, ctx_row, experts, ln_ffn_g[l], ln_ffn_b[l], _pick_tile(C, 256))
    return x
```
